```python
import math
import jax, jax.numpy as jnp
from jax import lax
import numpy as np

D_MODEL = 1024
BATCH = 2
SEQ = 8192
DEPTH = 4

N_EVEN = (DEPTH + 1) // 2
N_ODD = DEPTH // 2
ALPHA = (2 * DEPTH) ** 0.25
BETA = (8 * DEPTH) ** -0.25
LN_EPS = 1e-5
SC_WIDTH = D_MODEL
SC_KERNEL = 3
SSM_HEAD_DIM = 64
SSM_INNER = D_MODEL
SSM_HEADS = SSM_INNER // SSM_HEAD_DIM
SSM_GROUPS = 2
SSM_STATE = 128
SSM_CONV = 4
SSM_CHUNK = 128
SSM_CONV_DIM = SSM_INNER + 2 * SSM_GROUPS * SSM_STATE
IN_COLS = 3 * SC_WIDTH + SSM_INNER + SSM_CONV_DIM + SSM_HEADS
MIX_WIDTH = SC_WIDTH + SSM_INNER
CONF_KERNEL = 31
N_EXPERTS = 32
TOP_K = 4
D_EXPERT = D_MODEL
SWIGLU_LIMIT = 7.0
SWIGLU_ALPHA = 1.702
MOE_BLOCK = 512

kernel_name = "hybrid_shortconv_ssd_conformer_moe_deepnorm"


def layer_norm(x, g, b):
    xf = x.astype(jnp.float32)
    mu = jnp.mean(xf, axis=-1, keepdims=True)
    var = jnp.mean(jnp.square(xf - mu), axis=-1, keepdims=True)
    y = (xf - mu) * lax.rsqrt(var + LN_EPS) * g.astype(jnp.float32) + b.astype(jnp.float32)
    return y.astype(x.dtype)


def causal_dwconv(u, w):
    k, c = w.shape
    return lax.conv_general_dilated(
        u, w.astype(u.dtype)[:, None, :], window_strides=(1,),
        padding=[(k - 1, 0)], dimension_numbers=("NWC", "WIO", "NWC"),
        feature_group_count=c)


def ssd_chunked(x, dt, a, bm, cm):
    b, s, h, p = x.shape
    g, n = bm.shape[2], bm.shape[3]
    r = h // g
    c, l = s // SSM_CHUNK, SSM_CHUNK
    xd = (x * dt[..., None]).reshape(b, c, l, g, r, p)
    da = jnp.moveaxis((dt * a).reshape(b, c, l, g, r), 2, -1)
    a_cum = jnp.cumsum(da, axis=-1)
    bc = bm.reshape(b, c, l, g, n)
    cc = cm.reshape(b, c, l, g, n)
    seg = a_cum[..., :, None] - a_cum[..., None, :]
    causal = jnp.tril(jnp.ones((l, l), dtype=bool))
    decay_ls = jnp.exp(jnp.where(causal, seg, -jnp.inf))
    cb = jnp.einsum("bclgn,bcsgn->bcgls", cc, bc)
    y_diag = jnp.einsum("bcgls,bcgrls,bcsgrp->bclgrp", cb, decay_ls, xd)
    decay_to_end = jnp.exp(a_cum[..., -1:] - a_cum)
    states = jnp.einsum("bclgn,bcgrl,bclgrp->bcgrpn", bc, decay_to_end, xd)
    chunk_decay = jnp.exp(a_cum[..., -1])

    def step(carry, inp):
        st, dec = inp
        return carry * dec[..., None, None] + st, carry

    init = jnp.zeros((b, g, r, p, n), jnp.float32)
    _, prev = lax.scan(step, init, (jnp.moveaxis(states, 1, 0), jnp.moveaxis(chunk_decay, 1, 0)))
    prev = jnp.moveaxis(prev, 0, 1)
    y_off = jnp.einsum("bclgn,bcgrpn,bcgrl->bclgrp", cc, prev, jnp.exp(a_cum))
    return (y_diag + y_off).reshape(b, s, h, p)


def shortconv_ssd_mix(h, w_in, sc_conv_w, ssm_conv_w, ssm_conv_b, dt_bias, a_log, d_skip, norm_w, w_out):
    b, s, _ = h.shape
    proj = h @ w_in
    cuts = [SC_WIDTH, 2 * SC_WIDTH, 3 * SC_WIDTH, 3 * SC_WIDTH + SSM_INNER,
            3 * SC_WIDTH + SSM_INNER + SSM_CONV_DIM]
    sc_x, sc_pre, sc_post, z, xbc, dt_raw = jnp.split(proj, cuts, axis=-1)
    y_sc = sc_post * causal_dwconv(sc_pre * sc_x, sc_conv_w)
    xbc = jax.nn.silu(causal_dwconv(xbc, ssm_conv_w) + ssm_conv_b)
    gn = SSM_GROUPS * SSM_STATE
    xs, bm, cm = jnp.split(xbc, [SSM_INNER, SSM_INNER + gn], axis=-1)
    f32 = jnp.float32
    dt = jax.nn.softplus(dt_raw.astype(f32) + dt_bias.astype(f32))
    a = -jnp.exp(a_log.astype(f32))
    xh = xs.reshape(b, s, SSM_HEADS, SSM_HEAD_DIM).astype(f32)
    y = ssd_chunked(xh, dt, a,
                    bm.reshape(b, s, SSM_GROUPS, SSM_STATE).astype(f32),
                    cm.reshape(b, s, SSM_GROUPS, SSM_STATE).astype(f32))
    y = y + xh * d_skip.astype(f32)[:, None]
    y = y.reshape(b, s, SSM_INNER) * jax.nn.silu(z.astype(f32))
    yg = y.reshape(b, s, SSM_GROUPS, SSM_INNER // SSM_GROUPS)
    yg = yg * lax.rsqrt(jnp.mean(jnp.square(yg), axis=-1, keepdims=True) + LN_EPS)
    y_ssm = (yg.reshape(b, s, SSM_INNER) * norm_w.astype(f32)).astype(h.dtype)
    return jnp.concatenate([y_sc, y_ssm], axis=-1) @ w_out


def conformer_conv(h, w_pw1, b_pw1, w_dw, b_dw, ln_g, ln_b, w_pw2, b_pw2):
    u = h @ w_pw1 + b_pw1
    u = u[..., :D_MODEL] * jax.nn.sigmoid(u[..., D_MODEL:])
    u = causal_dwconv(u, w_dw) + b_dw
    u = jax.nn.silu(layer_norm(u, ln_g, ln_b))
    return u @ w_pw2 + b_pw2


def moe_ffn(h, w_r, b_r, w_gu, b_gu, w_down, b_down):
    bsz, s, d = h.shape
    t = bsz * s
    n_assign = t * TOP_K
    xt = h.reshape(t, d)
    logits = (xt @ w_r + b_r).astype(jnp.float32)
    top_vals, top_idx = lax.top_k(logits, TOP_K)
    gates = jax.nn.softmax(top_vals, axis=-1)
    flat_e = top_idx.reshape(-1)
    order = jnp.argsort(flat_e)
    sorted_e = flat_e[order]
    tok = order // TOP_K
    counts = jnp.bincount(flat_e, length=N_EXPERTS)
    starts = jnp.cumsum(counts) - counts
    padded = ((counts + MOE_BLOCK - 1) // MOE_BLOCK) * MOE_BLOCK
    pad_ends = jnp.cumsum(padded)
    pad_starts = pad_ends - padded
    dest = pad_starts[sorted_e] + (jnp.arange(n_assign) - starts[sorted_e])
    n_blocks = -(-n_assign // MOE_BLOCK) + N_EXPERTS
    buf = jnp.zeros((n_blocks * MOE_BLOCK, d), h.dtype).at[dest].set(xt[tok])
    block_e = jnp.minimum(
        jnp.searchsorted(pad_ends, jnp.arange(n_blocks) * MOE_BLOCK, side="right"),
        N_EXPERTS - 1)

    def expert_block(args):
        xb, e = args
        gu = xb @ w_gu[e] + b_gu[e]
        gate = jnp.minimum(gu[:, :D_EXPERT], SWIGLU_LIMIT)
        up = jnp.clip(gu[:, D_EXPERT:], -SWIGLU_LIMIT, SWIGLU_LIMIT)
        act = (up + 1.0) * (gate * jax.nn.sigmoid(SWIGLU_ALPHA * gate))
        return act @ w_down[e] + b_down[e]

    yb = lax.map(expert_block, (buf.reshape(n_blocks, MOE_BLOCK, d), block_e)).reshape(-1, d)
    y_assign = yb[dest] * gates.reshape(-1)[order].astype(h.dtype)[:, None]
    y = jax.ops.segment_sum(y_assign, tok, num_segments=t)
    return y.reshape(bsz, s, d)


def setup_inputs(seed: int = 0) -> dict:
    key = jax.random.key(seed)
    ks = jax.random.split(key, 32)
    f32 = jnp.float32
    nrm = lambda k, shape, scale: jax.random.normal(k, shape, f32) * scale
    dt0 = jnp.exp(jax.random.uniform(ks[6], (N_EVEN, SSM_HEADS), f32)
                  * (math.log(0.1) - math.log(0.001)) + math.log(0.001))
    return {
        "x": nrm(ks[0], (BATCH, SEQ, D_MODEL), 1.0),
        "mix_w_in": nrm(ks[1], (N_EVEN, D_MODEL, IN_COLS), D_MODEL ** -0.5),
        "sc_conv_w": nrm(ks[2], (N_EVEN, SC_KERNEL, SC_WIDTH), SC_KERNEL ** -0.5),
        "ssm_conv_w": nrm(ks[3], (N_EVEN, SSM_CONV, SSM_CONV_DIM), SSM_CONV ** -0.5),
        "ssm_conv_b": nrm(ks[4], (N_EVEN, SSM_CONV_DIM), 0.02),
        "ssm_dt_bias": dt0 + jnp.log(-jnp.expm1(-dt0)),
        "ssm_a_log": jnp.log(jax.random.uniform(ks[7], (N_EVEN, SSM_HEADS), f32, 1.0, 16.0)),
        "ssm_d": 1.0 + nrm(ks[8], (N_EVEN, SSM_HEADS), 0.1),
        "ssm_norm_w": 1.0 + nrm(ks[9], (N_EVEN, SSM_INNER), 0.1),
        "mix_w_out": nrm(ks[10], (N_EVEN, MIX_WIDTH, D_MODEL), BETA * MIX_WIDTH ** -0.5),
        "conf_w_pw1": nrm(ks[11], (N_ODD, D_MODEL, 2 * D_MODEL), D_MODEL ** -0.5),
        "conf_b_pw1": nrm(ks[12], (N_ODD, 2 * D_MODEL), 0.02),
        "conf_w_dw": nrm(ks[13], (N_ODD, CONF_KERNEL, D_MODEL), CONF_KERNEL ** -0.5),
        "conf_b_dw": nrm(ks[14], (N_ODD, D_MODEL), 0.02),
        "conf_ln_g": 1.0 + nrm(ks[15], (N_ODD, D_MODEL), 0.1),
        "conf_ln_b": nrm(ks[16], (N_ODD, D_MODEL), 0.02),
        "conf_w_pw2": nrm(ks[17], (N_ODD, D_MODEL, D_MODEL), BETA * D_MODEL ** -0.5),
        "conf_b_pw2": nrm(ks[18], (N_ODD, D_MODEL), 0.02),
        "router_w": nrm(ks[19], (DEPTH, D_MODEL, N_EXPERTS), D_MODEL ** -0.5),
        "router_b": nrm(ks[20], (DEPTH, N_EXPERTS), 0.01),
        "exp_w_gu": nrm(ks[21], (DEPTH, N_EXPERTS, D_MODEL, 2 * D_EXPERT), D_MODEL ** -0.5),
        "exp_b_gu": nrm(ks[22], (DEPTH, N_EXPERTS, 2 * D_EXPERT), 0.02),
        "exp_w_down": nrm(ks[23], (DEPTH, N_EXPERTS, D_EXPERT, D_MODEL), BETA * D_EXPERT ** -0.5),
        "exp_b_down": nrm(ks[24], (DEPTH, N_EXPERTS, D_MODEL), 0.02),
        "ln_mix_g": 1.0 + nrm(ks[25], (DEPTH, D_MODEL), 0.1),
        "ln_mix_b": nrm(ks[26], (DEPTH, D_MODEL), 0.02),
        "ln_ffn_g": 1.0 + nrm(ks[27], (DEPTH, D_MODEL), 0.1),
        "ln_ffn_b": nrm(ks[28], (DEPTH, D_MODEL), 0.02),
    }


def reference(x, mix_w_in, sc_conv_w, ssm_conv_w, ssm_conv_b, ssm_dt_bias, ssm_a_log, ssm_d,
              ssm_norm_w, mix_w_out, conf_w_pw1, conf_b_pw1, conf_w_dw, conf_b_dw, conf_ln_g,
              conf_ln_b, conf_w_pw2, conf_b_pw2, router_w, router_b, exp_w_gu, exp_b_gu,
              exp_w_down, exp_b_down, ln_mix_g, ln_mix_b, ln_ffn_g, ln_ffn_b):
    h = x
    for i in range(DEPTH):
        j = i // 2
        if i % 2 == 0:
            mix = shortconv_ssd_mix(h, mix_w_in[j], sc_conv_w[j], ssm_conv_w[j], ssm_conv_b[j],
                                    ssm_dt_bias[j], ssm_a_log[j], ssm_d[j], ssm_norm_w[j],
                                    mix_w_out[j])
        else:
            mix = conformer_conv(h, conf_w_pw1[j], conf_b_pw1[j], conf_w_dw[j], conf_b_dw[j],
                                 conf_ln_g[j], conf_ln_b[j], conf_w_pw2[j], conf_b_pw2[j])
        h = layer_norm(ALPHA * h + mix, ln_mix_g[i], ln_mix_b[i])
        ffn = moe_ffn(h, router_w[i], router_b[i], exp_w_gu[i], exp_b_gu[i],
                      exp_w_down[i], exp_b_down[i])
        h = layer_norm(ALPHA * h + ffn, ln_ffn_g[i], ln_ffn_b[i])
    return h
```

```python
import functools
import math

import jax
import jax.numpy as jnp
from jax import lax
from jax.experimental import pallas as pl
from jax.experimental.pallas import tpu as pltpu

F32 = jnp.float32
BF16 = jnp.bfloat16

D_MODEL = 1024
DEPTH = 4
ALPHA = (2 * DEPTH) ** 0.25
LN_EPS = 1e-5
SC_KERNEL = 3
SSM_HEAD_DIM = 64
SSM_HEADS = 16
SSM_GROUPS = 2
SSM_STATE = 128
SSM_CONV = 4
SSM_CHUNK = 128
SSM_INNER = D_MODEL
GROUP_COLS = SSM_INNER // SSM_GROUPS
XBC_COLS = SSM_INNER + 2 * SSM_GROUPS * SSM_STATE
CONF_KERNEL = 31
N_EXPERTS = 32
TOP_K = 4
SWIGLU_LIMIT = 7.0
SWIGLU_ALPHA = 1.702
MOE_BLOCK = 512

LANES = 128
SUBLANES = 8
CARRY_ROWS = 8
CONF_CARRY_ROWS = 32
VMEM_LIMIT = 56 * 1024 * 1024

MIX_ROWS = 256
ROUTER_ROWS = 512
TOKEN_ROWS = 128


def _layer_norm(v, g, b):
    mu = jnp.mean(v, axis=-1, keepdims=True)
    d = v - mu
    var = jnp.mean(d * d, axis=-1, keepdims=True)
    return d * lax.rsqrt(var + LN_EPS) * g + b


def _silu(v):
    return v * jax.nn.sigmoid(v)


def _softplus(v):
    return jnp.maximum(v, 0.0) + jnp.log1p(jnp.exp(-jnp.abs(v)))


def _dot(a, b):
    return jnp.dot(a, b, preferred_element_type=F32)


def _dot_exact01(m01, v):
    v1 = v.astype(BF16)
    r1 = v - v1.astype(F32)
    v2 = r1.astype(BF16)
    v3 = (r1 - v2.astype(F32)).astype(BF16)
    return _dot(m01, v1) + _dot(m01, v2) + _dot(m01, v3)


def _dot_exact01_rhs(v, m01):
    v1 = v.astype(BF16)
    r1 = v - v1.astype(F32)
    v2 = r1.astype(BF16)
    v3 = (r1 - v2.astype(F32)).astype(BF16)
    return _dot(v1, m01) + _dot(v2, m01) + _dot(v3, m01)


def _even_mixer_kernel(x_ref, w_sc_ref, w_z_ref, w_xbc_ref, w_dt_ref, scw_ref, ssmw_ref,
                       ssmb_ref, dtb_ref, a_ref, aexp_ref, dexp_ref, normw_ref, wout_ref,
                       lng_ref, lnb_ref, out_ref,
                       u_buf, xbc_buf, act_buf, dte_buf, dtn_buf, y_buf, state_ref):
    rows = x_ref.shape[1]
    c = pl.program_id(1)

    @pl.when(c == 0)
    def _():
        u_buf[0:CARRY_ROWS, :] = jnp.zeros((CARRY_ROWS, D_MODEL), F32)
        xbc_buf[0:CARRY_ROWS, :] = jnp.zeros((CARRY_ROWS, XBC_COLS), F32)
        state_ref[...] = jnp.zeros(state_ref.shape, F32)

    x = x_ref[0]
    xb = x.astype(BF16)

    p = _dot(xb, w_sc_ref[...])
    u = p[:, D_MODEL:2 * D_MODEL] * p[:, :D_MODEL]
    u_buf[CARRY_ROWS:CARRY_ROWS + rows, :] = u
    conv = scw_ref[2:3, :] * u
    for k in range(SC_KERNEL - 1):
        shift = SC_KERNEL - 1 - k
        conv = conv + scw_ref[k:k + 1, :] * u_buf[CARRY_ROWS - shift:CARRY_ROWS - shift + rows, :]
    y_sc = p[:, 2 * D_MODEL:] * conv
    u_buf[0:CARRY_ROWS, :] = u[rows - CARRY_ROWS:, :]

    xbc = _dot(xb, w_xbc_ref[...])
    xbc_buf[CARRY_ROWS:CARRY_ROWS + rows, :] = xbc
    cv = ssmw_ref[SSM_CONV - 1:SSM_CONV, :] * xbc + ssmb_ref[...]
    for k in range(SSM_CONV - 1):
        shift = SSM_CONV - 1 - k
        cv = cv + ssmw_ref[k:k + 1, :] * xbc_buf[CARRY_ROWS - shift:CARRY_ROWS - shift + rows, :]
    act_buf[...] = _silu(cv)
    xbc_buf[0:CARRY_ROWS, :] = xbc[rows - CARRY_ROWS:, :]

    dt_n = _softplus(_dot(xb, w_dt_ref[...]) + dtb_ref[...])
    dtn_buf[...] = dt_n
    lane_e = lax.broadcasted_iota(jnp.int32, (LANES, SSM_INNER), 1)
    row_e = lax.broadcasted_iota(jnp.int32, (LANES, SSM_INNER), 0)
    expand = jnp.where(lane_e // SSM_HEAD_DIM == row_e, 1.0, 0.0).astype(BF16)
    dte_buf[...] = _dot_exact01_rhs(dt_n, expand)

    li = lax.broadcasted_iota(jnp.int32, (SSM_CHUNK, SSM_CHUNK), 0)
    si = lax.broadcasted_iota(jnp.int32, (SSM_CHUNK, SSM_CHUNK), 1)
    causal = li >= si
    tril = jnp.where(causal, 1.0, 0.0).astype(BF16)
    low_half = si < SSM_HEAD_DIM

    for j in range(rows // SSM_CHUNK):
        r0 = j * SSM_CHUNK
        xs_c = act_buf[r0:r0 + SSM_CHUNK, 0:SSM_INNER]
        dt_c = dte_buf[r0:r0 + SSM_CHUNK, :]
        xd = xs_c * dt_c
        acum = _dot_exact01(tril, dt_c * aexp_ref[...])
        acum_n = _dot_exact01(tril, dtn_buf[r0:r0 + SSM_CHUNK, :] * a_ref[...])
        acum_t = acum_n.T
        alast = acum[SSM_CHUNK - 1:SSM_CHUNK, :]
        e_acum = jnp.exp(acum)
        xd_end = (xd * jnp.exp(alast - acum)).astype(BF16)
        chunk_decay = jnp.exp(alast)
        xdb = xd.astype(BF16)
        for g in range(SSM_GROUPS):
            b_g = act_buf[r0:r0 + SSM_CHUNK, SSM_INNER + g * SSM_STATE:SSM_INNER + (g + 1) * SSM_STATE]
            c_lo = SSM_INNER + (SSM_GROUPS + g) * SSM_STATE
            c_g = act_buf[r0:r0 + SSM_CHUNK, c_lo:c_lo + SSM_STATE].astype(BF16)
            cb = lax.dot_general(c_g, b_g.astype(BF16), (((1,), (1,)), ((), ())),
                                 preferred_element_type=F32)
            g_lo = g * GROUP_COLS
            st = state_ref[g]
            y_off = _dot(c_g, st.astype(BF16)) * e_acum[:, g_lo:g_lo + GROUP_COLS]
            for pair in range(GROUP_COLS // LANES):
                col0 = g_lo + pair * LANES
                xdp = xdb[:, col0:col0 + LANES]
                res = []
                for hh in range(2):
                    h = col0 // SSM_HEAD_DIM + hh
                    seg = acum_n[:, h:h + 1] - acum_t[h:h + 1, :]
                    decay = jnp.exp(jnp.where(causal, seg, -jnp.inf))
                    res.append(_dot((cb * decay).astype(BF16), xdp))
                y_pair = jnp.where(low_half, res[0], res[1])
                y_buf[r0:r0 + SSM_CHUNK, col0:col0 + LANES] = (
                    y_pair + y_off[:, pair * LANES:(pair + 1) * LANES])
            state_ref[g] = (st * chunk_decay[:, g_lo:g_lo + GROUP_COLS]
                            + _dot(b_g.T.astype(BF16), xd_end[:, g_lo:g_lo + GROUP_COLS]))

    z = _dot(xb, w_z_ref[...])
    y = (y_buf[...] + act_buf[:, 0:SSM_INNER] * dexp_ref[...]) * _silu(z)
    parts = []
    for g in range(SSM_GROUPS):
        yg = y[:, g * GROUP_COLS:(g + 1) * GROUP_COLS]
        ms = jnp.mean(yg * yg, axis=-1, keepdims=True)
        parts.append(yg * lax.rsqrt(ms + LN_EPS) * normw_ref[:, g * GROUP_COLS:(g + 1) * GROUP_COLS])
    mix = _dot(y_sc.astype(BF16), wout_ref[0:D_MODEL, :])
    for g in range(SSM_GROUPS):
        lo = D_MODEL + g * GROUP_COLS
        mix = mix + _dot(parts[g].astype(BF16), wout_ref[lo:lo + GROUP_COLS, :])
    out_ref[0] = _layer_norm(ALPHA * x + mix, lng_ref[...], lnb_ref[...])


def _const_spec(shape):
    nd = len(shape)
    return pl.BlockSpec(shape, lambda *_: (0,) * nd, pipeline_mode=pl.Buffered(1))


def _even_mixer(h, w_in, sc_conv_w, ssm_conv_w, ssm_conv_b, dt_bias, a_log, d_skip, norm_w,
                w_out, ln_g, ln_b):
    bsz, seq, _ = h.shape
    rows = min(MIX_ROWS, seq)
    cut_z = 3 * D_MODEL
    cut_xbc = cut_z + SSM_INNER
    cut_dt = cut_xbc + XBC_COLS
    w_sc = w_in[:, :cut_z].astype(BF16)
    w_z = w_in[:, cut_z:cut_xbc].astype(BF16)
    w_xbc = w_in[:, cut_xbc:cut_dt].astype(BF16)
    pad = LANES - SSM_HEADS
    w_dt = jnp.pad(w_in[:, cut_dt:], ((0, 0), (0, pad))).astype(BF16)
    dtb = jnp.pad(dt_bias, (0, pad)).reshape(1, LANES)
    a = -jnp.exp(a_log)
    a_n = jnp.pad(a, (0, pad)).reshape(1, LANES)
    a_exp = jnp.repeat(a, SSM_HEAD_DIM).reshape(1, SSM_INNER)
    d_exp = jnp.repeat(d_skip, SSM_HEAD_DIM).reshape(1, SSM_INNER)
    consts = [w_sc, w_z, w_xbc, w_dt, sc_conv_w, ssm_conv_w, ssm_conv_b.reshape(1, -1), dtb, a_n,
              a_exp, d_exp, norm_w.reshape(1, -1), w_out.astype(BF16), ln_g.reshape(1, -1),
              ln_b.reshape(1, -1)]
    x_spec = pl.BlockSpec((1, rows, D_MODEL), lambda b, c: (b, c, 0))
    return pl.pallas_call(
        _even_mixer_kernel,
        grid=(bsz, seq // rows),
        in_specs=[x_spec] + [_const_spec(v.shape) for v in consts],
        out_specs=x_spec,
        out_shape=jax.ShapeDtypeStruct(h.shape, F32),
        scratch_shapes=[
            pltpu.VMEM((CARRY_ROWS + rows, D_MODEL), F32),
            pltpu.VMEM((CARRY_ROWS + rows, XBC_COLS), F32),
            pltpu.VMEM((rows, XBC_COLS), F32),
            pltpu.VMEM((rows, SSM_INNER), F32),
            pltpu.VMEM((rows, LANES), F32),
            pltpu.VMEM((rows, SSM_INNER), F32),
            pltpu.VMEM((SSM_GROUPS, SSM_STATE, GROUP_COLS), F32),
        ],
        compiler_params=pltpu.CompilerParams(
            dimension_semantics=("arbitrary", "arbitrary"), vmem_limit_bytes=VMEM_LIMIT),
        name="even_mixer",
    )(h, *consts)


def _odd_mixer_kernel(x_ref, w1_ref, b1_ref, wdw_ref, bdw_ref, g_ref, b_ref, w2_ref, b2_ref,
                      lng_ref, lnb_ref, out_ref, u_buf):
    rows = x_ref.shape[1]
    c = pl.program_id(1)

    @pl.when(c == 0)
    def _():
        u_buf[0:CONF_CARRY_ROWS, :] = jnp.zeros((CONF_CARRY_ROWS, D_MODEL), F32)

    x = x_ref[0]
    u = _dot(x.astype(BF16), w1_ref[...]) + b1_ref[...]
    u = u[:, :D_MODEL] * jax.nn.sigmoid(u[:, D_MODEL:])
    u_buf[CONF_CARRY_ROWS:CONF_CARRY_ROWS + rows, :] = u
    conv = wdw_ref[CONF_KERNEL - 1:CONF_KERNEL, :] * u + bdw_ref[...]
    for k in range(CONF_KERNEL - 1):
        lo = CONF_CARRY_ROWS - (CONF_KERNEL - 1 - k)
        conv = conv + wdw_ref[k:k + 1, :] * u_buf[lo:lo + rows, :]
    u_buf[0:CONF_CARRY_ROWS, :] = u[rows - CONF_CARRY_ROWS:, :]
    v = _silu(_layer_norm(conv, g_ref[...], b_ref[...]))
    mix = _dot(v.astype(BF16), w2_ref[...]) + b2_ref[...]
    out_ref[0] = _layer_norm(ALPHA * x + mix, lng_ref[...], lnb_ref[...])


def _odd_mixer(h, w_pw1, b_pw1, w_dw, b_dw, ln_g_c, ln_b_c, w_pw2, b_pw2, ln_g, ln_b):
    bsz, seq, _ = h.shape
    rows = min(MIX_ROWS, seq)
    consts = [w_pw1.astype(BF16), b_pw1.reshape(1, -1), w_dw, b_dw.reshape(1, -1),
              ln_g_c.reshape(1, -1), ln_b_c.reshape(1, -1), w_pw2.astype(BF16),
              b_pw2.reshape(1, -1), ln_g.reshape(1, -1), ln_b.reshape(1, -1)]
    x_spec = pl.BlockSpec((1, rows, D_MODEL), lambda b, c: (b, c, 0))
    return pl.pallas_call(
        _odd_mixer_kernel,
        grid=(bsz, seq // rows),
        in_specs=[x_spec] + [_const_spec(v.shape) for v in consts],
        out_specs=x_spec,
        out_shape=jax.ShapeDtypeStruct(h.shape, F32),
        scratch_shapes=[pltpu.VMEM((CONF_CARRY_ROWS + rows, D_MODEL), F32)],
        compiler_params=pltpu.CompilerParams(
            dimension_semantics=("arbitrary", "arbitrary"), vmem_limit_bytes=VMEM_LIMIT),
        name="odd_mixer",
    )(h, *consts)


def _router_kernel(x_ref, w_ref, b_ref, idx_ref, gate_ref, rank_ref, cnt_ref, carry_ref):
    rows = x_ref.shape[0]
    i = pl.program_id(0)

    @pl.when(i == 0)
    def _():
        carry_ref[...] = jnp.zeros(carry_ref.shape, F32)

    logits = jnp.dot(x_ref[...], w_ref[...], precision=lax.Precision.HIGHEST,
                     preferred_element_type=F32) + b_ref[...]
    lane = lax.broadcasted_iota(jnp.int32, (rows, LANES), 1)
    lane_f = lane.astype(F32)
    work = logits
    vals, sels, idxs = [], [], []
    for _ in range(TOP_K):
        m = jnp.max(work, axis=-1, keepdims=True)
        first = jnp.min(jnp.where(work == m, lane_f, float(LANES)), axis=-1, keepdims=True)
        sel = lane_f == first
        work = jnp.where(sel, -jnp.inf, work)
        vals.append(m)
        sels.append(sel)
        idxs.append(first)
    exps = [jnp.exp(v - vals[0]) for v in vals]
    inv = 1.0 / (exps[0] + exps[1] + exps[2] + exps[3])
    onehot = jnp.zeros((rows, LANES), F32)
    for sel in sels:
        onehot = onehot + jnp.where(sel, 1.0, 0.0)
    ri = lax.broadcasted_iota(jnp.int32, (rows, rows), 0)
    ci = lax.broadcasted_iota(jnp.int32, (rows, rows), 1)
    strict = jnp.where(ri > ci, 1.0, 0.0).astype(BF16)
    before = _dot(strict, onehot.astype(BF16)) + carry_ref[0:1, :]
    idx_o = jnp.zeros((rows, LANES), F32)
    gate_o = jnp.zeros((rows, LANES), F32)
    rank_o = jnp.zeros((rows, LANES), F32)
    for k in range(TOP_K):
        rank_k = jnp.sum(jnp.where(sels[k], before, 0.0), axis=-1, keepdims=True)
        here = lane == k
        idx_o = jnp.where(here, idxs[k], idx_o)
        gate_o = jnp.where(here, exps[k] * inv, gate_o)
        rank_o = jnp.where(here, rank_k, rank_o)
    idx_ref[...] = idx_o.astype(jnp.int32)
    gate_ref[...] = gate_o
    rank_ref[...] = rank_o.astype(jnp.int32)
    total = carry_ref[0:1, :] + jnp.sum(onehot, axis=0, keepdims=True)
    carry_ref[...] = jnp.broadcast_to(total, carry_ref.shape)
    cnt_ref[...] = jnp.broadcast_to(total, cnt_ref.shape).astype(jnp.int32)


def _router(xt, w_r, b_r):
    t = xt.shape[0]
    rows = min(ROUTER_ROWS, t)
    pad = LANES - N_EXPERTS
    w = jnp.pad(w_r, ((0, 0), (0, pad)))
    b = jnp.pad(b_r, (0, pad), constant_values=-1e30).reshape(1, LANES)
    tok_spec = pl.BlockSpec((rows, LANES), lambda i: (i, 0))
    idx, gate, rank, cnt = pl.pallas_call(
        _router_kernel,
        grid=(t // rows,),
        in_specs=[pl.BlockSpec((rows, D_MODEL), lambda i: (i, 0)), _const_spec(w.shape),
                  _const_spec(b.shape)],
        out_specs=[tok_spec, tok_spec, tok_spec, pl.BlockSpec((SUBLANES, LANES), lambda i: (0, 0))],
        out_shape=[jax.ShapeDtypeStruct((t, LANES), jnp.int32),
                   jax.ShapeDtypeStruct((t, LANES), F32),
                   jax.ShapeDtypeStruct((t, LANES), jnp.int32),
                   jax.ShapeDtypeStruct((SUBLANES, LANES), jnp.int32)],
        scratch_shapes=[pltpu.VMEM((SUBLANES, LANES), F32)],
        compiler_params=pltpu.CompilerParams(dimension_semantics=("arbitrary",),
                                             vmem_limit_bytes=VMEM_LIMIT),
        name="moe_router",
    )(xt, w, b)
    return idx[:, :TOP_K], gate[:, :TOP_K], rank[:, :TOP_K], cnt[0, :N_EXPERTS]


def _row_copy(src_ref, src_row, dst_ref, dst_row, sem):
    return pltpu.make_async_copy(src_ref.at[pl.ds(src_row, 1), :],
                                 dst_ref.at[pl.ds(dst_row, 1), :], sem)


def _dispatch_kernel(dest_ref, x_ref, init_ref, xs_ref, sem):
    del init_ref
    rows = x_ref.shape[0]

    def issue(t, carry):
        for k in range(TOP_K):
            _row_copy(x_ref, t, xs_ref, dest_ref[0, 0, t * TOP_K + k], sem).start()
        return carry

    lax.fori_loop(0, rows, issue, 0)

    def drain(t, carry):
        for k in range(TOP_K):
            _row_copy(x_ref, 0, xs_ref, 0, sem).wait()
        return carry

    lax.fori_loop(0, rows, drain, 0)


def _dispatch(xt, dest, n_rows):
    t = xt.shape[0]
    rows = min(TOKEN_ROWS, t)
    dest3 = dest.reshape(t // rows, 1, rows * TOP_K)
    init = jnp.zeros((n_rows, D_MODEL), F32)
    return pl.pallas_call(
        _dispatch_kernel,
        grid=(t // rows,),
        in_specs=[pl.BlockSpec((1, 1, rows * TOP_K), lambda i: (i, 0, 0), memory_space=pltpu.SMEM),
                  pl.BlockSpec((rows, D_MODEL), lambda i: (i, 0)),
                  pl.BlockSpec(memory_space=pl.ANY)],
        out_specs=pl.BlockSpec(memory_space=pl.ANY),
        out_shape=jax.ShapeDtypeStruct((n_rows, D_MODEL), F32),
        scratch_shapes=[pltpu.SemaphoreType.DMA(())],
        input_output_aliases={2: 0},
        compiler_params=pltpu.CompilerParams(dimension_semantics=("arbitrary",)),
        name="moe_dispatch",
    )(dest3, xt, init)


def _expert_kernel(be_ref, nused_ref, x_ref, wgu_ref, bgu_ref, wd_ref, bd_ref, out_ref):
    del be_ref
    i = pl.program_id(0)

    @pl.when(i < nused_ref[0])
    def _():
        gu = _dot(x_ref[...].astype(BF16), wgu_ref[0].astype(BF16)) + bgu_ref[0]
        d_exp = wd_ref.shape[1]
        gate = jnp.minimum(gu[:, :d_exp], SWIGLU_LIMIT)
        up = jnp.clip(gu[:, d_exp:], -SWIGLU_LIMIT, SWIGLU_LIMIT)
        act = (up + 1.0) * (gate * jax.nn.sigmoid(SWIGLU_ALPHA * gate))
        out_ref[...] = _dot(act.astype(BF16), wd_ref[0].astype(BF16)) + bd_ref[0]

    @pl.when(i >= nused_ref[0])
    def _():
        out_ref[...] = jnp.zeros(out_ref.shape, F32)


def _experts(xs, block_e, n_used, w_gu, b_gu, w_down, b_down):
    n_rows = xs.shape[0]
    n_blocks = n_rows // MOE_BLOCK
    d_exp = w_down.shape[1]
    grid_spec = pltpu.PrefetchScalarGridSpec(
        num_scalar_prefetch=2,
        grid=(n_blocks,),
        in_specs=[
            pl.BlockSpec((MOE_BLOCK, D_MODEL), lambda i, be, nu: (i, 0)),
            pl.BlockSpec((1, D_MODEL, 2 * d_exp), lambda i, be, nu: (be[i], 0, 0)),
            pl.BlockSpec((1, 1, 2 * d_exp), lambda i, be, nu: (be[i], 0, 0)),
            pl.BlockSpec((1, d_exp, D_MODEL), lambda i, be, nu: (be[i], 0, 0)),
            pl.BlockSpec((1, 1, D_MODEL), lambda i, be, nu: (be[i], 0, 0)),
        ],
        out_specs=pl.BlockSpec((MOE_BLOCK, D_MODEL), lambda i, be, nu: (i, 0)),
    )
    return pl.pallas_call(
        _expert_kernel,
        grid_spec=grid_spec,
        out_shape=jax.ShapeDtypeStruct((n_rows, D_MODEL), F32),
        compiler_params=pltpu.CompilerParams(dimension_semantics=("arbitrary",),
                                             vmem_limit_bytes=VMEM_LIMIT),
        name="moe_experts",
    )(block_e, n_used, xs, w_gu, b_gu.reshape(N_EXPERTS, 1, -1), w_down,
      b_down.reshape(N_EXPERTS, 1, -1))


def _combine_kernel(dest_ref, h_ref, gate_ref, yb_ref, lng_ref, lnb_ref, out_ref, buf, sem):
    rows = h_ref.shape[0]

    def issue(t, carry):
        for k in range(TOP_K):
            _row_copy(yb_ref, dest_ref[0, 0, t * TOP_K + k], buf.at[k], t, sem).start()
        return carry

    lax.fori_loop(0, rows, issue, 0)

    def drain(t, carry):
        for k in range(TOP_K):
            _row_copy(yb_ref, 0, buf.at[k], 0, sem).wait()
        return carry

    lax.fori_loop(0, rows, drain, 0)

    gates = gate_ref[...]
    y = gates[:, 0:1] * buf[0]
    for k in range(1, TOP_K):
        y = y + gates[:, k:k + 1] * buf[k]
    out_ref[...] = _layer_norm(ALPHA * h_ref[...] + y, lng_ref[...], lnb_ref[...])


def _combine(xt, gates, dest, yb, ln_g, ln_b):
    t = xt.shape[0]
    rows = min(TOKEN_ROWS, t)
    dest3 = dest.reshape(t // rows, 1, rows * TOP_K)
    tok_spec = pl.BlockSpec((rows, D_MODEL), lambda i: (i, 0))
    return pl.pallas_call(
        _combine_kernel,
        grid=(t // rows,),
        in_specs=[pl.BlockSpec((1, 1, rows * TOP_K), lambda i: (i, 0, 0), memory_space=pltpu.SMEM),
                  tok_spec,
                  pl.BlockSpec((rows, TOP_K), lambda i: (i, 0)),
                  pl.BlockSpec(memory_space=pl.ANY),
                  _const_spec((1, D_MODEL)), _const_spec((1, D_MODEL))],
        out_specs=tok_spec,
        out_shape=jax.ShapeDtypeStruct((t, D_MODEL), F32),
        scratch_shapes=[pltpu.VMEM((TOP_K, rows, D_MODEL), F32), pltpu.SemaphoreType.DMA(())],
        compiler_params=pltpu.CompilerParams(dimension_semantics=("arbitrary",)),
        name="moe_combine",
    )(dest3, xt, gates, yb, ln_g.reshape(1, -1), ln_b.reshape(1, -1))


def _moe_layer(h, w_r, b_r, w_gu, b_gu, w_down, b_down, ln_g, ln_b):
    bsz, seq, d = h.shape
    t = bsz * seq
    xt = h.reshape(t, d)
    idx, gates, rank, counts = _router(xt, w_r, b_r)
    padded = ((counts + MOE_BLOCK - 1) // MOE_BLOCK) * MOE_BLOCK
    pad_ends = jnp.cumsum(padded)
    pad_starts = pad_ends - padded
    dest = (pad_starts[idx] + rank).astype(jnp.int32)
    n_blocks = -(-(t * TOP_K) // MOE_BLOCK) + N_EXPERTS
    block_e = jnp.minimum(
        jnp.searchsorted(pad_ends, jnp.arange(n_blocks, dtype=jnp.int32) * MOE_BLOCK, side="right"),
        N_EXPERTS - 1).astype(jnp.int32)
    n_used = (pad_ends[-1:] // MOE_BLOCK).astype(jnp.int32)
    xs = _dispatch(xt, dest, n_blocks * MOE_BLOCK)
    yb = _experts(xs, block_e, n_used, w_gu, b_gu, w_down, b_down)
    out = _combine(xt, gates, dest, yb, ln_g, ln_b)
    return out.reshape(bsz, seq, d)


def kernel(x, mix_w_in, sc_conv_w, ssm_conv_w, ssm_conv_b, ssm_dt_bias, ssm_a_log, ssm_d, ssm_norm_w, mix_w_out, conf_w_pw1, conf_b_pw1, conf_w_dw, conf_b_dw, conf_ln_g, conf_ln_b, conf_w_pw2, conf_b_pw2, router_w, router_b, exp_w_gu, exp_b_gu, exp_w_down, exp_b_down, ln_mix_g, ln_mix_b, ln_ffn_g, ln_ffn_b):
    h = x
    for i in range(DEPTH):
        j = i // 2
        if i % 2 == 0:
            h = _even_mixer(h, mix_w_in[j], sc_conv_w[j], ssm_conv_w[j], ssm_conv_b[j],
                            ssm_dt_bias[j], ssm_a_log[j], ssm_d[j], ssm_norm_w[j], mix_w_out[j],
                            ln_mix_g[i], ln_mix_b[i])
        else:
            h = _odd_mixer(h, conf_w_pw1[j], conf_b_pw1[j], conf_w_dw[j], conf_b_dw[j],
                           conf_ln_g[j], conf_ln_b[j], conf_w_pw2[j], conf_b_pw2[j],
                           ln_mix_g[i], ln_mix_b[i])
        h = _moe_layer(h, router_w[i], router_b[i], exp_w_gu[i], exp_b_gu[i], exp_w_down[i],
                       exp_b_down[i], ln_ffn_g[i], ln_ffn_b[i])
    return h
```

```python
import functools
import math

import jax
import jax.numpy as jnp
from jax import lax
from jax.experimental import pallas as pl
from jax.experimental.pallas import tpu as pltpu

F32 = jnp.float32
BF16 = jnp.bfloat16

D_MODEL = 1024
DEPTH = 4
ALPHA = (2 * DEPTH) ** 0.25
LN_EPS = 1e-5
SC_KERNEL = 3
SSM_HEAD_DIM = 64
SSM_HEADS = 16
SSM_GROUPS = 2
SSM_STATE = 128
SSM_CONV = 4
SSM_CHUNK = 128
SSM_INNER = D_MODEL
GROUP_COLS = SSM_INNER // SSM_GROUPS
XBC_COLS = SSM_INNER + 2 * SSM_GROUPS * SSM_STATE
CONF_KERNEL = 31
N_EXPERTS = 32
TOP_K = 4
SWIGLU_LIMIT = 7.0
SWIGLU_ALPHA = 1.702
MOE_BLOCK = 512

LANES = 128
SUBLANES = 8
CARRY_ROWS = 8
CONF_CARRY_ROWS = 32
CONF_TILE_ROWS = 32
VMEM_LIMIT = 56 * 1024 * 1024

MIX_ROWS = 256
ROUTER_ROWS = 512
TOKEN_ROWS = 128


def _layer_norm(v, g, b):
    mu = jnp.mean(v, axis=-1, keepdims=True)
    d = v - mu
    var = jnp.mean(d * d, axis=-1, keepdims=True)
    return d * lax.rsqrt(var + LN_EPS) * g + b


def _silu(v):
    return v * jax.nn.sigmoid(v)


def _softplus(v):
    return jnp.maximum(v, 0.0) + jnp.log1p(jnp.exp(-jnp.abs(v)))


def _dot(a, b):
    return jnp.dot(a, b, preferred_element_type=F32)


def _dot_exact01(m01, v):
    v1 = v.astype(BF16)
    r1 = v - v1.astype(F32)
    v2 = r1.astype(BF16)
    v3 = (r1 - v2.astype(F32)).astype(BF16)
    return _dot(m01, v1) + _dot(m01, v2) + _dot(m01, v3)


def _even_mixer_kernel(x_ref, w_sc_ref, w_z_ref, w_xbc_ref, w_dt_ref, scw_ref, ssmw_ref,
                       ssmb_ref, dtb_ref, a_ref, dexp_ref, normw_ref, wout_ref,
                       lng_ref, lnb_ref, out_ref,
                       u_buf, xbc_buf, act_buf, dtn_buf, y_buf, state_ref):
    rows = x_ref.shape[1]
    c = pl.program_id(1)

    @pl.when(c == 0)
    def _():
        u_buf[0:CARRY_ROWS, :] = jnp.zeros((CARRY_ROWS, D_MODEL), F32)
        xbc_buf[0:CARRY_ROWS, :] = jnp.zeros((CARRY_ROWS, XBC_COLS), F32)
        state_ref[...] = jnp.zeros(state_ref.shape, F32)

    x = x_ref[0]
    xb = x.astype(BF16)

    p = _dot(xb, w_sc_ref[...])
    u = p[:, D_MODEL:2 * D_MODEL] * p[:, :D_MODEL]
    u_buf[CARRY_ROWS:CARRY_ROWS + rows, :] = u
    conv = scw_ref[2:3, :] * u
    for k in range(SC_KERNEL - 1):
        shift = SC_KERNEL - 1 - k
        conv = conv + scw_ref[k:k + 1, :] * u_buf[CARRY_ROWS - shift:CARRY_ROWS - shift + rows, :]
    y_sc = p[:, 2 * D_MODEL:] * conv
    u_buf[0:CARRY_ROWS, :] = u[rows - CARRY_ROWS:, :]

    xbc = _dot(xb, w_xbc_ref[...])
    xbc_buf[CARRY_ROWS:CARRY_ROWS + rows, :] = xbc
    cv = ssmw_ref[SSM_CONV - 1:SSM_CONV, :] * xbc + ssmb_ref[...]
    for k in range(SSM_CONV - 1):
        shift = SSM_CONV - 1 - k
        cv = cv + ssmw_ref[k:k + 1, :] * xbc_buf[CARRY_ROWS - shift:CARRY_ROWS - shift + rows, :]
    act_buf[...] = _silu(cv)
    xbc_buf[0:CARRY_ROWS, :] = xbc[rows - CARRY_ROWS:, :]

    dtn_buf[...] = _softplus(_dot(xb, w_dt_ref[...]) + dtb_ref[...])

    li = lax.broadcasted_iota(jnp.int32, (SSM_CHUNK, SSM_CHUNK), 0)
    si = lax.broadcasted_iota(jnp.int32, (SSM_CHUNK, SSM_CHUNK), 1)
    causal = li >= si
    tril = jnp.where(causal, 1.0, 0.0).astype(BF16)
    low_half = si < SSM_HEAD_DIM
    low_row = low_half[0:1, :]

    for j in range(rows // SSM_CHUNK):
        r0 = j * SSM_CHUNK
        dt_c = dtn_buf[r0:r0 + SSM_CHUNK, :]
        acum = _dot_exact01(tril, dt_c * a_ref[...])
        alast = acum[SSM_CHUNK - 1:SSM_CHUNK, :]
        acum_t = acum.T
        dt_t = dt_c.T
        e_acum = jnp.exp(acum)
        w_t = (jnp.exp(alast - acum) * dt_c).T
        chunk_decay = jnp.exp(alast)
        xs_b = act_buf[r0:r0 + SSM_CHUNK, 0:SSM_INNER].astype(BF16)
        for g in range(SSM_GROUPS):
            b_g = act_buf[r0:r0 + SSM_CHUNK, SSM_INNER + g * SSM_STATE:SSM_INNER + (g + 1) * SSM_STATE]
            c_lo = SSM_INNER + (SSM_GROUPS + g) * SSM_STATE
            c_g = act_buf[r0:r0 + SSM_CHUNK, c_lo:c_lo + SSM_STATE]
            cb = lax.dot_general(c_g.astype(BF16), b_g.astype(BF16), (((1,), (1,)), ((), ())),
                                 preferred_element_type=F32)
            b_gt = b_g.T
            for pair in range(GROUP_COLS // LANES):
                col0 = g * GROUP_COLS + pair * LANES
                h0 = col0 // SSM_HEAD_DIM
                xs_p = xs_b[:, col0:col0 + LANES]
                zero_b = jnp.zeros_like(xs_p)
                xs_lo = jnp.where(low_half, xs_p, zero_b)
                xs_hi = jnp.where(low_half, zero_b, xs_p)
                st = state_ref[g, :, pair * LANES:(pair + 1) * LANES]
                st_b = st.astype(BF16)
                lhs, lhs_s = [], []
                for h in (h0, h0 + 1):
                    seg = acum[:, h:h + 1] - acum_t[h:h + 1, :]
                    decay = jnp.exp(jnp.where(causal, seg, -jnp.inf))
                    lhs.append((cb * decay * dt_t[h:h + 1, :]).astype(BF16))
                    lhs_s.append((b_gt * w_t[h:h + 1, :]).astype(BF16))
                for h in (h0, h0 + 1):
                    lhs.append((c_g * e_acum[:, h:h + 1]).astype(BF16))
                rhs = jnp.concatenate([xs_lo, xs_hi, jnp.where(low_half, st_b, zero_b),
                                       jnp.where(low_half, zero_b, st_b)], axis=0)
                y_buf[r0:r0 + SSM_CHUNK, col0:col0 + LANES] = _dot(
                    jnp.concatenate(lhs, axis=1), rhs)
                cd_p = jnp.where(low_row, chunk_decay[:, h0:h0 + 1], chunk_decay[:, h0 + 1:h0 + 2])
                state_ref[g, :, pair * LANES:(pair + 1) * LANES] = st * cd_p + _dot(
                    jnp.concatenate(lhs_s, axis=1), jnp.concatenate([xs_lo, xs_hi], axis=0))

    z = _dot(xb, w_z_ref[...])
    y = (y_buf[...] + act_buf[:, 0:SSM_INNER] * dexp_ref[...]) * _silu(z)
    parts = []
    for g in range(SSM_GROUPS):
        yg = y[:, g * GROUP_COLS:(g + 1) * GROUP_COLS]
        ms = jnp.mean(yg * yg, axis=-1, keepdims=True)
        parts.append(yg * lax.rsqrt(ms + LN_EPS) * normw_ref[:, g * GROUP_COLS:(g + 1) * GROUP_COLS])
    mix = _dot(y_sc.astype(BF16), wout_ref[0:D_MODEL, :])
    for g in range(SSM_GROUPS):
        lo = D_MODEL + g * GROUP_COLS
        mix = mix + _dot(parts[g].astype(BF16), wout_ref[lo:lo + GROUP_COLS, :])
    out_ref[0] = _layer_norm(ALPHA * x + mix, lng_ref[...], lnb_ref[...])


def _const_spec(shape):
    nd = len(shape)
    return pl.BlockSpec(shape, lambda *_: (0,) * nd, pipeline_mode=pl.Buffered(1))


def _even_mixer(h, w_in, sc_conv_w, ssm_conv_w, ssm_conv_b, dt_bias, a_log, d_skip, norm_w,
                w_out, ln_g, ln_b):
    bsz, seq, _ = h.shape
    rows = min(MIX_ROWS, seq)
    cut_z = 3 * D_MODEL
    cut_xbc = cut_z + SSM_INNER
    cut_dt = cut_xbc + XBC_COLS
    w_sc = w_in[:, :cut_z].astype(BF16)
    w_z = w_in[:, cut_z:cut_xbc].astype(BF16)
    w_xbc = w_in[:, cut_xbc:cut_dt].astype(BF16)
    pad = LANES - SSM_HEADS
    w_dt = jnp.pad(w_in[:, cut_dt:], ((0, 0), (0, pad))).astype(BF16)
    dtb = jnp.pad(dt_bias, (0, pad)).reshape(1, LANES)
    a = -jnp.exp(a_log)
    a_n = jnp.pad(a, (0, pad)).reshape(1, LANES)
    d_exp = jnp.repeat(d_skip, SSM_HEAD_DIM).reshape(1, SSM_INNER)
    consts = [w_sc, w_z, w_xbc, w_dt, sc_conv_w, ssm_conv_w, ssm_conv_b.reshape(1, -1), dtb, a_n,
              d_exp, norm_w.reshape(1, -1), w_out.astype(BF16), ln_g.reshape(1, -1),
              ln_b.reshape(1, -1)]
    x_spec = pl.BlockSpec((1, rows, D_MODEL), lambda b, c: (b, c, 0))
    return pl.pallas_call(
        _even_mixer_kernel,
        grid=(bsz, seq // rows),
        in_specs=[x_spec] + [_const_spec(v.shape) for v in consts],
        out_specs=x_spec,
        out_shape=jax.ShapeDtypeStruct(h.shape, F32),
        scratch_shapes=[
            pltpu.VMEM((CARRY_ROWS + rows, D_MODEL), F32),
            pltpu.VMEM((CARRY_ROWS + rows, XBC_COLS), F32),
            pltpu.VMEM((rows, XBC_COLS), F32),
            pltpu.VMEM((rows, LANES), F32),
            pltpu.VMEM((rows, SSM_INNER), F32),
            pltpu.VMEM((SSM_GROUPS, SSM_STATE, GROUP_COLS), F32),
        ],
        compiler_params=pltpu.CompilerParams(
            dimension_semantics=("arbitrary", "arbitrary"), vmem_limit_bytes=VMEM_LIMIT),
        name="even_mixer",
    )(h, *consts)


def _odd_mixer_kernel(x_ref, w1_ref, b1_ref, wdw_ref, bdw_ref, g_ref, b_ref, w2_ref, b2_ref,
                      lng_ref, lnb_ref, out_ref, sh_buf, conv_buf):
    rows = x_ref.shape[1]
    c = pl.program_id(1)

    @pl.when(c == 0)
    def _():
        sh_buf[0, 0:CONF_CARRY_ROWS, :] = jnp.zeros((CONF_CARRY_ROWS, D_MODEL), F32)

    x = x_ref[0]
    u = _dot(x.astype(BF16), w1_ref[...]) + b1_ref[...]
    u = u[:, :D_MODEL] * jax.nn.sigmoid(u[:, D_MODEL:])
    sh_buf[0, CONF_CARRY_ROWS:CONF_CARRY_ROWS + rows, :] = u
    span = rows + CONF_CARRY_ROWS - SUBLANES
    for r in range(1, SUBLANES):
        sh_buf[r, 0:span, :] = sh_buf[0, r:r + span, :]

    def conv_tile(i, carry):
        base = pl.multiple_of(i * CONF_TILE_ROWS, CONF_TILE_ROWS)
        groups = CONF_TILE_ROWS // SUBLANES
        acc = jnp.broadcast_to(bdw_ref[...][None], (groups, SUBLANES, D_MODEL))
        for k in range(CONF_KERNEL):
            lo = CONF_CARRY_ROWS - (CONF_KERNEL - 1 - k)
            tap = sh_buf[lo % SUBLANES, pl.ds(base + (lo // SUBLANES) * SUBLANES, CONF_TILE_ROWS), :]
            acc = acc + wdw_ref[k][None] * tap.reshape(groups, SUBLANES, D_MODEL)
        conv_buf[pl.ds(base, CONF_TILE_ROWS), :] = acc.reshape(CONF_TILE_ROWS, D_MODEL)
        return carry

    lax.fori_loop(0, rows // CONF_TILE_ROWS, conv_tile, 0)
    sh_buf[0, 0:CONF_CARRY_ROWS, :] = u[rows - CONF_CARRY_ROWS:, :]
    v = _silu(_layer_norm(conv_buf[...], g_ref[...], b_ref[...]))
    mix = _dot(v.astype(BF16), w2_ref[...]) + b2_ref[...]
    out_ref[0] = _layer_norm(ALPHA * x + mix, lng_ref[...], lnb_ref[...])


def _odd_mixer(h, w_pw1, b_pw1, w_dw, b_dw, ln_g_c, ln_b_c, w_pw2, b_pw2, ln_g, ln_b):
    bsz, seq, _ = h.shape
    rows = min(MIX_ROWS, seq)
    w_dw_rows = jnp.broadcast_to(w_dw[:, None, :], (CONF_KERNEL, SUBLANES, D_MODEL))
    b_dw_rows = jnp.broadcast_to(b_dw[None, :], (SUBLANES, D_MODEL))
    consts = [w_pw1.astype(BF16), b_pw1.reshape(1, -1), w_dw_rows, b_dw_rows,
              ln_g_c.reshape(1, -1), ln_b_c.reshape(1, -1), w_pw2.astype(BF16),
              b_pw2.reshape(1, -1), ln_g.reshape(1, -1), ln_b.reshape(1, -1)]
    x_spec = pl.BlockSpec((1, rows, D_MODEL), lambda b, c: (b, c, 0))
    return pl.pallas_call(
        _odd_mixer_kernel,
        grid=(bsz, seq // rows),
        in_specs=[x_spec] + [_const_spec(v.shape) for v in consts],
        out_specs=x_spec,
        out_shape=jax.ShapeDtypeStruct(h.shape, F32),
        scratch_shapes=[pltpu.VMEM((SUBLANES, CONF_CARRY_ROWS + rows, D_MODEL), F32),
                        pltpu.VMEM((rows, D_MODEL), F32)],
        compiler_params=pltpu.CompilerParams(
            dimension_semantics=("arbitrary", "arbitrary"), vmem_limit_bytes=VMEM_LIMIT),
        name="odd_mixer",
    )(h, *consts)


def _router_kernel(x_ref, w_ref, b_ref, idx_ref, gate_ref, rank_ref, cnt_ref, carry_ref):
    rows = x_ref.shape[0]
    i = pl.program_id(0)

    @pl.when(i == 0)
    def _():
        carry_ref[...] = jnp.zeros(carry_ref.shape, F32)

    logits = jnp.dot(x_ref[...], w_ref[...], precision=lax.Precision.HIGHEST,
                     preferred_element_type=F32) + b_ref[...]
    lane = lax.broadcasted_iota(jnp.int32, (rows, LANES), 1)
    lane_f = lane.astype(F32)
    work = logits
    vals, sels, idxs = [], [], []
    for _ in range(TOP_K):
        m = jnp.max(work, axis=-1, keepdims=True)
        first = jnp.min(jnp.where(work == m, lane_f, float(LANES)), axis=-1, keepdims=True)
        sel = lane_f == first
        work = jnp.where(sel, -jnp.inf, work)
        vals.append(m)
        sels.append(sel)
        idxs.append(first)
    exps = [jnp.exp(v - vals[0]) for v in vals]
    inv = 1.0 / (exps[0] + exps[1] + exps[2] + exps[3])
    onehot = jnp.zeros((rows, LANES), F32)
    for sel in sels:
        onehot = onehot + jnp.where(sel, 1.0, 0.0)
    ri = lax.broadcasted_iota(jnp.int32, (rows, rows), 0)
    ci = lax.broadcasted_iota(jnp.int32, (rows, rows), 1)
    strict = jnp.where(ri > ci, 1.0, 0.0).astype(BF16)
    before = _dot(strict, onehot.astype(BF16)) + carry_ref[0:1, :]
    idx_o = jnp.zeros((rows, LANES), F32)
    gate_o = jnp.zeros((rows, LANES), F32)
    rank_o = jnp.zeros((rows, LANES), F32)
    for k in range(TOP_K):
        rank_k = jnp.sum(jnp.where(sels[k], before, 0.0), axis=-1, keepdims=True)
        here = lane == k
        idx_o = jnp.where(here, idxs[k], idx_o)
        gate_o = jnp.where(here, exps[k] * inv, gate_o)
        rank_o = jnp.where(here, rank_k, rank_o)
    idx_ref[...] = idx_o.astype(jnp.int32)
    gate_ref[...] = gate_o
    rank_ref[...] = rank_o.astype(jnp.int32)
    total = carry_ref[0:1, :] + jnp.sum(onehot, axis=0, keepdims=True)
    carry_ref[...] = jnp.broadcast_to(total, carry_ref.shape)
    cnt_ref[...] = jnp.broadcast_to(total, cnt_ref.shape).astype(jnp.int32)


def _router(xt, w_r, b_r):
    t = xt.shape[0]
    rows = min(ROUTER_ROWS, t)
    pad = LANES - N_EXPERTS
    w = jnp.pad(w_r, ((0, 0), (0, pad)))
    b = jnp.pad(b_r, (0, pad), constant_values=-1e30).reshape(1, LANES)
    tok_spec = pl.BlockSpec((rows, LANES), lambda i: (i, 0))
    idx, gate, rank, cnt = pl.pallas_call(
        _router_kernel,
        grid=(t // rows,),
        in_specs=[pl.BlockSpec((rows, D_MODEL), lambda i: (i, 0)), _const_spec(w.shape),
                  _const_spec(b.shape)],
        out_specs=[tok_spec, tok_spec, tok_spec, pl.BlockSpec((SUBLANES, LANES), lambda i: (0, 0))],
        out_shape=[jax.ShapeDtypeStruct((t, LANES), jnp.int32),
                   jax.ShapeDtypeStruct((t, LANES), F32),
                   jax.ShapeDtypeStruct((t, LANES), jnp.int32),
                   jax.ShapeDtypeStruct((SUBLANES, LANES), jnp.int32)],
        scratch_shapes=[pltpu.VMEM((SUBLANES, LANES), F32)],
        compiler_params=pltpu.CompilerParams(dimension_semantics=("arbitrary",),
                                             vmem_limit_bytes=VMEM_LIMIT),
        name="moe_router",
    )(xt, w, b)
    return idx[:, :TOP_K], gate[:, :TOP_K], rank[:, :TOP_K], cnt[0, :N_EXPERTS]


def _row_copy(src_ref, src_row, dst_ref, dst_row, sem):
    return pltpu.make_async_copy(src_ref.at[pl.ds(src_row, 1), :],
                                 dst_ref.at[pl.ds(dst_row, 1), :], sem)


def _dispatch_kernel(dest_ref, x_ref, init_ref, xs_ref, sem):
    del init_ref
    rows = x_ref.shape[0]

    def issue(t, carry):
        for k in range(TOP_K):
            _row_copy(x_ref, t, xs_ref, dest_ref[0, 0, t * TOP_K + k], sem).start()
        return carry

    lax.fori_loop(0, rows, issue, 0)

    def drain(t, carry):
        for k in range(TOP_K):
            _row_copy(x_ref, 0, xs_ref, 0, sem).wait()
        return carry

    lax.fori_loop(0, rows, drain, 0)


def _dispatch(xt, dest, n_rows):
    t = xt.shape[0]
    rows = min(TOKEN_ROWS, t)
    dest3 = dest.reshape(t // rows, 1, rows * TOP_K)
    init = jnp.zeros((n_rows, D_MODEL), F32)
    return pl.pallas_call(
        _dispatch_kernel,
        grid=(t // rows,),
        in_specs=[pl.BlockSpec((1, 1, rows * TOP_K), lambda i: (i, 0, 0), memory_space=pltpu.SMEM),
                  pl.BlockSpec((rows, D_MODEL), lambda i: (i, 0)),
                  pl.BlockSpec(memory_space=pl.ANY)],
        out_specs=pl.BlockSpec(memory_space=pl.ANY),
        out_shape=jax.ShapeDtypeStruct((n_rows, D_MODEL), F32),
        scratch_shapes=[pltpu.SemaphoreType.DMA(())],
        input_output_aliases={2: 0},
        compiler_params=pltpu.CompilerParams(dimension_semantics=("arbitrary",)),
        name="moe_dispatch",
    )(dest3, xt, init)


def _expert_kernel(be_ref, nused_ref, x_ref, wgu_ref, bgu_ref, wd_ref, bd_ref, out_ref):
    del be_ref
    i = pl.program_id(0)

    @pl.when(i < nused_ref[0])
    def _():
        gu = _dot(x_ref[...].astype(BF16), wgu_ref[0, 0].astype(BF16)) + bgu_ref[0, 0]
        d_exp = wd_ref.shape[2]
        gate = jnp.minimum(gu[:, :d_exp], SWIGLU_LIMIT)
        up = jnp.clip(gu[:, d_exp:], -SWIGLU_LIMIT, SWIGLU_LIMIT)
        act = (up + 1.0) * (gate * jax.nn.sigmoid(SWIGLU_ALPHA * gate))
        out_ref[...] = _dot(act.astype(BF16), wd_ref[0, 0].astype(BF16)) + bd_ref[0, 0]

    @pl.when(i >= nused_ref[0])
    def _():
        out_ref[...] = jnp.zeros(out_ref.shape, F32)


def _experts(xs, block_e, n_used, layer, w_gu, b_gu, w_down, b_down):
    n_rows = xs.shape[0]
    n_blocks = n_rows // MOE_BLOCK
    depth, n_exp, d_exp = w_down.shape[:3]
    grid_spec = pltpu.PrefetchScalarGridSpec(
        num_scalar_prefetch=2,
        grid=(n_blocks,),
        in_specs=[
            pl.BlockSpec((MOE_BLOCK, D_MODEL), lambda i, be, nu: (i, 0)),
            pl.BlockSpec((1, 1, D_MODEL, 2 * d_exp), lambda i, be, nu: (layer, be[i], 0, 0)),
            pl.BlockSpec((1, 1, 1, 2 * d_exp), lambda i, be, nu: (layer, be[i], 0, 0)),
            pl.BlockSpec((1, 1, d_exp, D_MODEL), lambda i, be, nu: (layer, be[i], 0, 0)),
            pl.BlockSpec((1, 1, 1, D_MODEL), lambda i, be, nu: (layer, be[i], 0, 0)),
        ],
        out_specs=pl.BlockSpec((MOE_BLOCK, D_MODEL), lambda i, be, nu: (i, 0)),
    )
    return pl.pallas_call(
        _expert_kernel,
        grid_spec=grid_spec,
        out_shape=jax.ShapeDtypeStruct((n_rows, D_MODEL), F32),
        compiler_params=pltpu.CompilerParams(dimension_semantics=("arbitrary",),
                                             vmem_limit_bytes=VMEM_LIMIT),
        name="moe_experts",
    )(block_e, n_used, xs, w_gu, b_gu.reshape(depth, n_exp, 1, -1), w_down,
      b_down.reshape(depth, n_exp, 1, -1))


def _combine_kernel(dest_ref, h_ref, gate_ref, yb_ref, lng_ref, lnb_ref, out_ref, buf, sem):
    rows = h_ref.shape[0]

    def issue(t, carry):
        for k in range(TOP_K):
            _row_copy(yb_ref, dest_ref[0, 0, t * TOP_K + k], buf.at[k], t, sem).start()
        return carry

    lax.fori_loop(0, rows, issue, 0)

    def drain(t, carry):
        for k in range(TOP_K):
            _row_copy(yb_ref, 0, buf.at[k], 0, sem).wait()
        return carry

    lax.fori_loop(0, rows, drain, 0)

    gates = gate_ref[...]
    y = gates[:, 0:1] * buf[0]
    for k in range(1, TOP_K):
        y = y + gates[:, k:k + 1] * buf[k]
    out_ref[...] = _layer_norm(ALPHA * h_ref[...] + y, lng_ref[...], lnb_ref[...])


def _combine(xt, gates, dest, yb, ln_g, ln_b):
    t = xt.shape[0]
    rows = min(TOKEN_ROWS, t)
    dest3 = dest.reshape(t // rows, 1, rows * TOP_K)
    tok_spec = pl.BlockSpec((rows, D_MODEL), lambda i: (i, 0))
    return pl.pallas_call(
        _combine_kernel,
        grid=(t // rows,),
        in_specs=[pl.BlockSpec((1, 1, rows * TOP_K), lambda i: (i, 0, 0), memory_space=pltpu.SMEM),
                  tok_spec,
                  pl.BlockSpec((rows, TOP_K), lambda i: (i, 0)),
                  pl.BlockSpec(memory_space=pl.ANY),
                  _const_spec((1, D_MODEL)), _const_spec((1, D_MODEL))],
        out_specs=tok_spec,
        out_shape=jax.ShapeDtypeStruct((t, D_MODEL), F32),
        scratch_shapes=[pltpu.VMEM((TOP_K, rows, D_MODEL), F32), pltpu.SemaphoreType.DMA(())],
        compiler_params=pltpu.CompilerParams(dimension_semantics=("arbitrary",)),
        name="moe_combine",
    )(dest3, xt, gates, yb, ln_g.reshape(1, -1), ln_b.reshape(1, -1))


def _moe_layer(h, layer, w_r, b_r, w_gu, b_gu, w_down, b_down, ln_g, ln_b):
    bsz, seq, d = h.shape
    t = bsz * seq
    xt = h.reshape(t, d)
    idx, gates, rank, counts = _router(xt, w_r, b_r)
    padded = ((counts + MOE_BLOCK - 1) // MOE_BLOCK) * MOE_BLOCK
    pad_ends = jnp.cumsum(padded)
    pad_starts = pad_ends - padded
    dest = (pad_starts[idx] + rank).astype(jnp.int32)
    n_blocks = -(-(t * TOP_K) // MOE_BLOCK) + N_EXPERTS
    block_lo = jnp.arange(n_blocks, dtype=jnp.int32) * MOE_BLOCK
    block_e = jnp.minimum(jnp.sum(pad_ends[None, :] <= block_lo[:, None], axis=1),
                          N_EXPERTS - 1).astype(jnp.int32)
    n_used = (pad_ends[-1:] // MOE_BLOCK).astype(jnp.int32)
    xs = _dispatch(xt, dest, n_blocks * MOE_BLOCK)
    yb = _experts(xs, block_e, n_used, layer, w_gu, b_gu, w_down, b_down)
    out = _combine(xt, gates, dest, yb, ln_g, ln_b)
    return out.reshape(bsz, seq, d)


def kernel(x, mix_w_in, sc_conv_w, ssm_conv_w, ssm_conv_b, ssm_dt_bias, ssm_a_log, ssm_d, ssm_norm_w, mix_w_out, conf_w_pw1, conf_b_pw1, conf_w_dw, conf_b_dw, conf_ln_g, conf_ln_b, conf_w_pw2, conf_b_pw2, router_w, router_b, exp_w_gu, exp_b_gu, exp_w_down, exp_b_down, ln_mix_g, ln_mix_b, ln_ffn_g, ln_ffn_b):
    h = x
    for i in range(DEPTH):
        j = i // 2
        if i % 2 == 0:
            h = _even_mixer(h, mix_w_in[j], sc_conv_w[j], ssm_conv_w[j], ssm_conv_b[j],
                            ssm_dt_bias[j], ssm_a_log[j], ssm_d[j], ssm_norm_w[j], mix_w_out[j],
                            ln_mix_g[i], ln_mix_b[i])
        else:
            h = _odd_mixer(h, conf_w_pw1[j], conf_b_pw1[j], conf_w_dw[j], conf_b_dw[j],
                           conf_ln_g[j], conf_ln_b[j], conf_w_pw2[j], conf_b_pw2[j],
                           ln_mix_g[i], ln_mix_b[i])
        h = _moe_layer(h, i, router_w[i], router_b[i], exp_w_gu, exp_b_gu, exp_w_down,
                       exp_b_down, ln_ffn_g[i], ln_ffn_b[i])
    return h
```

```python
import functools
import math

import jax
import jax.numpy as jnp
from jax import lax
from jax.experimental import pallas as pl
from jax.experimental.pallas import tpu as pltpu
from jax.experimental.pallas import tpu_sc as plsc

F32 = jnp.float32
BF16 = jnp.bfloat16

D_MODEL = 1024
DEPTH = 4
ALPHA = (2 * DEPTH) ** 0.25
LN_EPS = 1e-5
SC_KERNEL = 3
SSM_HEAD_DIM = 64
SSM_HEADS = 16
SSM_GROUPS = 2
SSM_STATE = 128
SSM_CONV = 4
SSM_CHUNK = 128
SSM_INNER = D_MODEL
GROUP_COLS = SSM_INNER // SSM_GROUPS
XBC_COLS = SSM_INNER + 2 * SSM_GROUPS * SSM_STATE
CONF_KERNEL = 31
N_EXPERTS = 32
TOP_K = 4
SWIGLU_LIMIT = 7.0
SWIGLU_ALPHA = 1.702
MOE_BLOCK = 512

LANES = 128
SUBLANES = 8
CARRY_ROWS = 8
CONF_CARRY_ROWS = 32
CONF_TILE_ROWS = 32
VMEM_LIMIT = 56 * 1024 * 1024

MIX_ROWS = 256
ROUTER_ROWS = 512
COMBINE_ROWS = 256
SC_ROWS = 32


def _layer_norm(v, g, b):
    mu = jnp.mean(v, axis=-1, keepdims=True)
    d = v - mu
    var = jnp.mean(d * d, axis=-1, keepdims=True)
    return d * lax.rsqrt(var + LN_EPS) * g + b


def _silu(v):
    return v * jax.nn.sigmoid(v)


def _softplus(v):
    return jnp.maximum(v, 0.0) + jnp.log1p(jnp.exp(-jnp.abs(v)))


def _dot(a, b):
    return jnp.dot(a, b, preferred_element_type=F32)


def _dot_exact01(m01, v):
    v1 = v.astype(BF16)
    r1 = v - v1.astype(F32)
    v2 = r1.astype(BF16)
    v3 = (r1 - v2.astype(F32)).astype(BF16)
    return _dot(m01, v1) + _dot(m01, v2) + _dot(m01, v3)


def _even_mixer_kernel(x_ref, w_sc_ref, w_z_ref, w_xbc_ref, w_dt_ref, scw_ref, ssmw_ref,
                       ssmb_ref, dtb_ref, a_ref, dexp_ref, normw_ref, wout_ref,
                       lng_ref, lnb_ref, out_ref,
                       u_buf, xbc_buf, act_buf, dtn_buf, y_buf, state_ref):
    rows = x_ref.shape[1]
    c = pl.program_id(1)

    @pl.when(c == 0)
    def _():
        u_buf[0:CARRY_ROWS, :] = jnp.zeros((CARRY_ROWS, D_MODEL), F32)
        xbc_buf[0:CARRY_ROWS, :] = jnp.zeros((CARRY_ROWS, XBC_COLS), F32)
        state_ref[...] = jnp.zeros(state_ref.shape, F32)

    x = x_ref[0]
    xb = x.astype(BF16)

    p = _dot(xb, w_sc_ref[...])
    u = p[:, D_MODEL:2 * D_MODEL] * p[:, :D_MODEL]
    u_buf[CARRY_ROWS:CARRY_ROWS + rows, :] = u
    conv = scw_ref[2:3, :] * u
    for k in range(SC_KERNEL - 1):
        shift = SC_KERNEL - 1 - k
        conv = conv + scw_ref[k:k + 1, :] * u_buf[CARRY_ROWS - shift:CARRY_ROWS - shift + rows, :]
    y_sc = p[:, 2 * D_MODEL:] * conv
    u_buf[0:CARRY_ROWS, :] = u[rows - CARRY_ROWS:, :]

    xbc = _dot(xb, w_xbc_ref[...])
    xbc_buf[CARRY_ROWS:CARRY_ROWS + rows, :] = xbc
    cv = ssmw_ref[SSM_CONV - 1:SSM_CONV, :] * xbc + ssmb_ref[...]
    for k in range(SSM_CONV - 1):
        shift = SSM_CONV - 1 - k
        cv = cv + ssmw_ref[k:k + 1, :] * xbc_buf[CARRY_ROWS - shift:CARRY_ROWS - shift + rows, :]
    act_buf[...] = _silu(cv)
    xbc_buf[0:CARRY_ROWS, :] = xbc[rows - CARRY_ROWS:, :]

    dtn_buf[...] = _softplus(_dot(xb, w_dt_ref[...]) + dtb_ref[...])

    li = lax.broadcasted_iota(jnp.int32, (SSM_CHUNK, SSM_CHUNK), 0)
    si = lax.broadcasted_iota(jnp.int32, (SSM_CHUNK, SSM_CHUNK), 1)
    causal = li >= si
    tril = jnp.where(causal, 1.0, 0.0).astype(BF16)
    low_half = si < SSM_HEAD_DIM
    low_row = low_half[0:1, :]

    for j in range(rows // SSM_CHUNK):
        r0 = j * SSM_CHUNK
        dt_c = dtn_buf[r0:r0 + SSM_CHUNK, :]
        acum = _dot_exact01(tril, dt_c * a_ref[...])
        alast = acum[SSM_CHUNK - 1:SSM_CHUNK, :]
        acum_t = acum.T
        dt_t = dt_c.T
        e_acum = jnp.exp(acum)
        w_t = (jnp.exp(alast - acum) * dt_c).T
        chunk_decay = jnp.exp(alast)
        xs_b = act_buf[r0:r0 + SSM_CHUNK, 0:SSM_INNER].astype(BF16)
        for g in range(SSM_GROUPS):
            b_g = act_buf[r0:r0 + SSM_CHUNK, SSM_INNER + g * SSM_STATE:SSM_INNER + (g + 1) * SSM_STATE]
            c_lo = SSM_INNER + (SSM_GROUPS + g) * SSM_STATE
            c_g = act_buf[r0:r0 + SSM_CHUNK, c_lo:c_lo + SSM_STATE]
            cb = lax.dot_general(c_g.astype(BF16), b_g.astype(BF16), (((1,), (1,)), ((), ())),
                                 preferred_element_type=F32)
            b_gt = b_g.T
            for pair in range(GROUP_COLS // LANES):
                col0 = g * GROUP_COLS + pair * LANES
                h0 = col0 // SSM_HEAD_DIM
                xs_p = xs_b[:, col0:col0 + LANES]
                zero_b = jnp.zeros_like(xs_p)
                xs_lo = jnp.where(low_half, xs_p, zero_b)
                xs_hi = jnp.where(low_half, zero_b, xs_p)
                st = state_ref[g, :, pair * LANES:(pair + 1) * LANES]
                st_b = st.astype(BF16)
                lhs, lhs_s = [], []
                for h in (h0, h0 + 1):
                    seg = acum[:, h:h + 1] - acum_t[h:h + 1, :]
                    decay = jnp.exp(jnp.where(causal, seg, -jnp.inf))
                    lhs.append((cb * decay * dt_t[h:h + 1, :]).astype(BF16))
                    lhs_s.append((b_gt * w_t[h:h + 1, :]).astype(BF16))
                for h in (h0, h0 + 1):
                    lhs.append((c_g * e_acum[:, h:h + 1]).astype(BF16))
                rhs = jnp.concatenate([xs_lo, xs_hi, jnp.where(low_half, st_b, zero_b),
                                       jnp.where(low_half, zero_b, st_b)], axis=0)
                y_buf[r0:r0 + SSM_CHUNK, col0:col0 + LANES] = _dot(
                    jnp.concatenate(lhs, axis=1), rhs)
                cd_p = jnp.where(low_row, chunk_decay[:, h0:h0 + 1], chunk_decay[:, h0 + 1:h0 + 2])
                state_ref[g, :, pair * LANES:(pair + 1) * LANES] = st * cd_p + _dot(
                    jnp.concatenate(lhs_s, axis=1), jnp.concatenate([xs_lo, xs_hi], axis=0))

    z = _dot(xb, w_z_ref[...])
    y = (y_buf[...] + act_buf[:, 0:SSM_INNER] * dexp_ref[...]) * _silu(z)
    parts = []
    for g in range(SSM_GROUPS):
        yg = y[:, g * GROUP_COLS:(g + 1) * GROUP_COLS]
        ms = jnp.mean(yg * yg, axis=-1, keepdims=True)
        parts.append(yg * lax.rsqrt(ms + LN_EPS) * normw_ref[:, g * GROUP_COLS:(g + 1) * GROUP_COLS])
    mix = _dot(y_sc.astype(BF16), wout_ref[0:D_MODEL, :])
    for g in range(SSM_GROUPS):
        lo = D_MODEL + g * GROUP_COLS
        mix = mix + _dot(parts[g].astype(BF16), wout_ref[lo:lo + GROUP_COLS, :])
    out_ref[0] = _layer_norm(ALPHA * x + mix, lng_ref[...], lnb_ref[...])


def _const_spec(shape):
    nd = len(shape)
    return pl.BlockSpec(shape, lambda *_: (0,) * nd, pipeline_mode=pl.Buffered(1))


def _even_mixer(h, w_in, sc_conv_w, ssm_conv_w, ssm_conv_b, dt_bias, a_log, d_skip, norm_w,
                w_out, ln_g, ln_b):
    bsz, seq, _ = h.shape
    rows = min(MIX_ROWS, seq)
    cut_z = 3 * D_MODEL
    cut_xbc = cut_z + SSM_INNER
    cut_dt = cut_xbc + XBC_COLS
    w_sc = w_in[:, :cut_z].astype(BF16)
    w_z = w_in[:, cut_z:cut_xbc].astype(BF16)
    w_xbc = w_in[:, cut_xbc:cut_dt].astype(BF16)
    pad = LANES - SSM_HEADS
    w_dt = jnp.pad(w_in[:, cut_dt:], ((0, 0), (0, pad))).astype(BF16)
    dtb = jnp.pad(dt_bias, (0, pad)).reshape(1, LANES)
    a = -jnp.exp(a_log)
    a_n = jnp.pad(a, (0, pad)).reshape(1, LANES)
    d_exp = jnp.repeat(d_skip, SSM_HEAD_DIM).reshape(1, SSM_INNER)
    consts = [w_sc, w_z, w_xbc, w_dt, sc_conv_w, ssm_conv_w, ssm_conv_b.reshape(1, -1), dtb, a_n,
              d_exp, norm_w.reshape(1, -1), w_out.astype(BF16), ln_g.reshape(1, -1),
              ln_b.reshape(1, -1)]
    x_spec = pl.BlockSpec((1, rows, D_MODEL), lambda b, c: (b, c, 0))
    return pl.pallas_call(
        _even_mixer_kernel,
        grid=(bsz, seq // rows),
        in_specs=[x_spec] + [_const_spec(v.shape) for v in consts],
        out_specs=x_spec,
        out_shape=jax.ShapeDtypeStruct(h.shape, F32),
        scratch_shapes=[
            pltpu.VMEM((CARRY_ROWS + rows, D_MODEL), F32),
            pltpu.VMEM((CARRY_ROWS + rows, XBC_COLS), F32),
            pltpu.VMEM((rows, XBC_COLS), F32),
            pltpu.VMEM((rows, LANES), F32),
            pltpu.VMEM((rows, SSM_INNER), F32),
            pltpu.VMEM((SSM_GROUPS, SSM_STATE, GROUP_COLS), F32),
        ],
        compiler_params=pltpu.CompilerParams(
            dimension_semantics=("arbitrary", "arbitrary"), vmem_limit_bytes=VMEM_LIMIT),
        name="even_mixer",
    )(h, *consts)


def _odd_mixer_kernel(x_ref, w1_ref, b1_ref, wdw_ref, bdw_ref, g_ref, b_ref, w2_ref, b2_ref,
                      lng_ref, lnb_ref, out_ref, sh_buf, conv_buf):
    rows = x_ref.shape[1]
    c = pl.program_id(1)

    @pl.when(c == 0)
    def _():
        sh_buf[0, 0:CONF_CARRY_ROWS, :] = jnp.zeros((CONF_CARRY_ROWS, D_MODEL), F32)

    x = x_ref[0]
    u = _dot(x.astype(BF16), w1_ref[...]) + b1_ref[...]
    u = u[:, :D_MODEL] * jax.nn.sigmoid(u[:, D_MODEL:])
    sh_buf[0, CONF_CARRY_ROWS:CONF_CARRY_ROWS + rows, :] = u
    span = rows + CONF_CARRY_ROWS - SUBLANES
    for r in range(1, SUBLANES):
        sh_buf[r, 0:span, :] = sh_buf[0, r:r + span, :]

    def conv_tile(i, carry):
        base = pl.multiple_of(i * CONF_TILE_ROWS, CONF_TILE_ROWS)
        groups = CONF_TILE_ROWS // SUBLANES
        acc = jnp.broadcast_to(bdw_ref[...][None], (groups, SUBLANES, D_MODEL))
        for k in range(CONF_KERNEL):
            lo = CONF_CARRY_ROWS - (CONF_KERNEL - 1 - k)
            tap = sh_buf[lo % SUBLANES, pl.ds(base + (lo // SUBLANES) * SUBLANES, CONF_TILE_ROWS), :]
            acc = acc + wdw_ref[k][None] * tap.reshape(groups, SUBLANES, D_MODEL)
        conv_buf[pl.ds(base, CONF_TILE_ROWS), :] = acc.reshape(CONF_TILE_ROWS, D_MODEL)
        return carry

    lax.fori_loop(0, rows // CONF_TILE_ROWS, conv_tile, 0)
    sh_buf[0, 0:CONF_CARRY_ROWS, :] = u[rows - CONF_CARRY_ROWS:, :]
    v = _silu(_layer_norm(conv_buf[...], g_ref[...], b_ref[...]))
    mix = _dot(v.astype(BF16), w2_ref[...]) + b2_ref[...]
    out_ref[0] = _layer_norm(ALPHA * x + mix, lng_ref[...], lnb_ref[...])


def _odd_mixer(h, w_pw1, b_pw1, w_dw, b_dw, ln_g_c, ln_b_c, w_pw2, b_pw2, ln_g, ln_b):
    bsz, seq, _ = h.shape
    rows = min(MIX_ROWS, seq)
    w_dw_rows = jnp.broadcast_to(w_dw[:, None, :], (CONF_KERNEL, SUBLANES, D_MODEL))
    b_dw_rows = jnp.broadcast_to(b_dw[None, :], (SUBLANES, D_MODEL))
    consts = [w_pw1.astype(BF16), b_pw1.reshape(1, -1), w_dw_rows, b_dw_rows,
              ln_g_c.reshape(1, -1), ln_b_c.reshape(1, -1), w_pw2.astype(BF16),
              b_pw2.reshape(1, -1), ln_g.reshape(1, -1), ln_b.reshape(1, -1)]
    x_spec = pl.BlockSpec((1, rows, D_MODEL), lambda b, c: (b, c, 0))
    return pl.pallas_call(
        _odd_mixer_kernel,
        grid=(bsz, seq // rows),
        in_specs=[x_spec] + [_const_spec(v.shape) for v in consts],
        out_specs=x_spec,
        out_shape=jax.ShapeDtypeStruct(h.shape, F32),
        scratch_shapes=[pltpu.VMEM((SUBLANES, CONF_CARRY_ROWS + rows, D_MODEL), F32),
                        pltpu.VMEM((rows, D_MODEL), F32)],
        compiler_params=pltpu.CompilerParams(
            dimension_semantics=("arbitrary", "arbitrary"), vmem_limit_bytes=VMEM_LIMIT),
        name="odd_mixer",
    )(h, *consts)


def _router_kernel(x_ref, w_ref, b_ref, idx_ref, gate_ref, rank_ref, cnt_ref, carry_ref):
    rows = x_ref.shape[0]
    i = pl.program_id(0)

    @pl.when(i == 0)
    def _():
        carry_ref[...] = jnp.zeros(carry_ref.shape, F32)

    logits = jnp.dot(x_ref[...], w_ref[...], precision=lax.Precision.HIGHEST,
                     preferred_element_type=F32) + b_ref[...]
    lane = lax.broadcasted_iota(jnp.int32, (rows, LANES), 1)
    lane_f = lane.astype(F32)
    work = logits
    vals, sels, idxs = [], [], []
    for _ in range(TOP_K):
        m = jnp.max(work, axis=-1, keepdims=True)
        first = jnp.min(jnp.where(work == m, lane_f, float(LANES)), axis=-1, keepdims=True)
        sel = lane_f == first
        work = jnp.where(sel, -jnp.inf, work)
        vals.append(m)
        sels.append(sel)
        idxs.append(first)
    exps = [jnp.exp(v - vals[0]) for v in vals]
    inv = 1.0 / (exps[0] + exps[1] + exps[2] + exps[3])
    onehot = jnp.zeros((rows, LANES), F32)
    for sel in sels:
        onehot = onehot + jnp.where(sel, 1.0, 0.0)
    ri = lax.broadcasted_iota(jnp.int32, (rows, rows), 0)
    ci = lax.broadcasted_iota(jnp.int32, (rows, rows), 1)
    strict = jnp.where(ri > ci, 1.0, 0.0).astype(BF16)
    before = _dot(strict, onehot.astype(BF16)) + carry_ref[0:1, :]
    idx_o = jnp.zeros((rows, LANES), F32)
    gate_o = jnp.zeros((rows, LANES), F32)
    rank_o = jnp.zeros((rows, LANES), F32)
    for k in range(TOP_K):
        rank_k = jnp.sum(jnp.where(sels[k], before, 0.0), axis=-1, keepdims=True)
        here = lane == k
        idx_o = jnp.where(here, idxs[k], idx_o)
        gate_o = jnp.where(here, exps[k] * inv, gate_o)
        rank_o = jnp.where(here, rank_k, rank_o)
    idx_ref[...] = idx_o.astype(jnp.int32)
    gate_ref[...] = gate_o
    rank_ref[...] = rank_o.astype(jnp.int32)
    total = carry_ref[0:1, :] + jnp.sum(onehot, axis=0, keepdims=True)
    carry_ref[...] = jnp.broadcast_to(total, carry_ref.shape)
    cnt_ref[...] = jnp.broadcast_to(total, cnt_ref.shape).astype(jnp.int32)


def _router(xt, w_r, b_r):
    t = xt.shape[0]
    rows = min(ROUTER_ROWS, t)
    pad = LANES - N_EXPERTS
    w = jnp.pad(w_r, ((0, 0), (0, pad)))
    b = jnp.pad(b_r, (0, pad), constant_values=-1e30).reshape(1, LANES)
    tok_spec = pl.BlockSpec((rows, LANES), lambda i: (i, 0))
    idx, gate, rank, cnt = pl.pallas_call(
        _router_kernel,
        grid=(t // rows,),
        in_specs=[pl.BlockSpec((rows, D_MODEL), lambda i: (i, 0)), _const_spec(w.shape),
                  _const_spec(b.shape)],
        out_specs=[tok_spec, tok_spec, tok_spec, pl.BlockSpec((SUBLANES, LANES), lambda i: (0, 0))],
        out_shape=[jax.ShapeDtypeStruct((t, LANES), jnp.int32),
                   jax.ShapeDtypeStruct((t, LANES), F32),
                   jax.ShapeDtypeStruct((t, LANES), jnp.int32),
                   jax.ShapeDtypeStruct((SUBLANES, LANES), jnp.int32)],
        scratch_shapes=[pltpu.VMEM((SUBLANES, LANES), F32)],
        compiler_params=pltpu.CompilerParams(dimension_semantics=("arbitrary",),
                                             vmem_limit_bytes=VMEM_LIMIT),
        name="moe_router",
    )(xt, w, b)
    return idx[:, :TOP_K], gate[:, :TOP_K], rank[:, :TOP_K], cnt[0, :N_EXPERTS]


def _sc_workers():
    mesh = plsc.VectorSubcoreMesh(core_axis_name="c", subcore_axis_name="s")
    return mesh, mesh.num_cores, mesh.num_cores * mesh.num_subcores


def _dispatch(xt, dest_km, n_rows):
    t = xt.shape[0]
    mesh, n_cores, n_workers = _sc_workers()
    n_ch = t // n_workers // SC_ROWS
    assert n_ch * SC_ROWS * n_workers == t and n_ch % 2 == 0

    def body(x_hbm, dest_hbm, xs_hbm, idx_v, rows_v, load_sem, scat_sem):
        wid = lax.axis_index("s") * n_cores + lax.axis_index("c")
        base = wid * (n_ch * SC_ROWS)
        for k in range(TOP_K):
            pltpu.sync_copy(dest_hbm.at[pl.ds(k * (t // SC_ROWS) + wid * n_ch, n_ch)], idx_v.at[k])

        def load(j, b):
            off = pl.multiple_of(base + j * SC_ROWS, SC_ROWS)
            return pltpu.make_async_copy(x_hbm.at[pl.ds(off, SC_ROWS)], rows_v.at[b], load_sem.at[b])

        def scatter(j, b, k):
            return pltpu.make_async_copy(rows_v.at[b], xs_hbm.at[idx_v.at[k, j]], scat_sem.at[b])

        load(0, 0).start()

        @pl.loop(0, n_ch, step=2)
        def _(j0):
            for b in range(2):
                j = j0 + b
                load(j, b).wait()

                @pl.when(j >= 1)
                def _():
                    for k in range(TOP_K):
                        scatter(j - 1, 1 - b, k).wait()

                @pl.when(j + 1 < n_ch)
                def _():
                    load(j + 1, 1 - b).start()

                for k in range(TOP_K):
                    scatter(j, b, k).start()

        for k in range(TOP_K):
            scatter(n_ch - 1, 1, k).wait()

    return pl.kernel(
        body, mesh=mesh,
        out_type=jax.ShapeDtypeStruct((n_rows, D_MODEL), F32),
        scratch_types=[pltpu.VMEM((TOP_K, n_ch, SC_ROWS), jnp.int32),
                       pltpu.VMEM((2, SC_ROWS, D_MODEL), F32),
                       pltpu.SemaphoreType.DMA((2,)), pltpu.SemaphoreType.DMA((2,))],
    )(xt, dest_km)


def _gather_rows(yb, dest_km):
    n_idx = dest_km.shape[0] * SC_ROWS
    mesh, n_cores, n_workers = _sc_workers()
    n_ch = n_idx // n_workers // SC_ROWS
    assert n_ch * SC_ROWS * n_workers == n_idx and n_ch % 2 == 0

    def body(yb_hbm, dest_hbm, out_hbm, idx_v, rows_v, gat_sem, store_sem):
        wid = lax.axis_index("s") * n_cores + lax.axis_index("c")
        base = wid * (n_ch * SC_ROWS)
        pltpu.sync_copy(dest_hbm.at[pl.ds(wid * n_ch, n_ch)], idx_v)

        def gather(j, b):
            return pltpu.make_async_copy(yb_hbm.at[idx_v.at[j]], rows_v.at[b], gat_sem.at[b])

        def store(j, b):
            off = pl.multiple_of(base + j * SC_ROWS, SC_ROWS)
            return pltpu.make_async_copy(rows_v.at[b], out_hbm.at[pl.ds(off, SC_ROWS)], store_sem.at[b])

        gather(0, 0).start()

        @pl.loop(0, n_ch, step=2)
        def _(j0):
            for b in range(2):
                j = j0 + b
                gather(j, b).wait()

                @pl.when(j >= 1)
                def _():
                    store(j - 1, 1 - b).wait()

                @pl.when(j + 1 < n_ch)
                def _():
                    gather(j + 1, 1 - b).start()

                store(j, b).start()

        store(n_ch - 1, 1).wait()

    return pl.kernel(
        body, mesh=mesh,
        out_type=jax.ShapeDtypeStruct((n_idx, D_MODEL), F32),
        scratch_types=[pltpu.VMEM((n_ch, SC_ROWS), jnp.int32),
                       pltpu.VMEM((2, SC_ROWS, D_MODEL), F32),
                       pltpu.SemaphoreType.DMA((2,)), pltpu.SemaphoreType.DMA((2,))],
    )(yb, dest_km)


def _expert_kernel(be_ref, nused_ref, valid_ref, x_ref, wgu_ref, bgu_ref, wd_ref, bd_ref, out_ref):
    del be_ref
    i = pl.program_id(0)

    @pl.when(i < nused_ref[0])
    def _():
        row = lax.broadcasted_iota(jnp.int32, x_ref.shape, 0)
        x = jnp.where(row < valid_ref[i], x_ref[...], 0.0)
        gu = _dot(x.astype(BF16), wgu_ref[0, 0].astype(BF16)) + bgu_ref[0, 0]
        d_exp = wd_ref.shape[2]
        gate = jnp.minimum(gu[:, :d_exp], SWIGLU_LIMIT)
        up = jnp.clip(gu[:, d_exp:], -SWIGLU_LIMIT, SWIGLU_LIMIT)
        act = (up + 1.0) * (gate * jax.nn.sigmoid(SWIGLU_ALPHA * gate))
        out_ref[...] = _dot(act.astype(BF16), wd_ref[0, 0].astype(BF16)) + bd_ref[0, 0]

    @pl.when(i >= nused_ref[0])
    def _():
        out_ref[...] = jnp.zeros(out_ref.shape, F32)


def _experts(xs, block_e, n_used, valid, layer, w_gu, b_gu, w_down, b_down):
    n_rows = xs.shape[0]
    n_blocks = n_rows // MOE_BLOCK
    depth, n_exp, d_exp = w_down.shape[:3]
    grid_spec = pltpu.PrefetchScalarGridSpec(
        num_scalar_prefetch=3,
        grid=(n_blocks,),
        in_specs=[
            pl.BlockSpec((MOE_BLOCK, D_MODEL), lambda i, be, nu, va: (jnp.minimum(i, nu[0] - 1), 0)),
            pl.BlockSpec((1, 1, D_MODEL, 2 * d_exp), lambda i, be, nu, va: (layer, be[i], 0, 0)),
            pl.BlockSpec((1, 1, 1, 2 * d_exp), lambda i, be, nu, va: (layer, be[i], 0, 0)),
            pl.BlockSpec((1, 1, d_exp, D_MODEL), lambda i, be, nu, va: (layer, be[i], 0, 0)),
            pl.BlockSpec((1, 1, 1, D_MODEL), lambda i, be, nu, va: (layer, be[i], 0, 0)),
        ],
        out_specs=pl.BlockSpec((MOE_BLOCK, D_MODEL), lambda i, be, nu, va: (i, 0)),
    )
    return pl.pallas_call(
        _expert_kernel,
        grid_spec=grid_spec,
        out_shape=jax.ShapeDtypeStruct((n_rows, D_MODEL), F32),
        compiler_params=pltpu.CompilerParams(dimension_semantics=("arbitrary",),
                                             vmem_limit_bytes=VMEM_LIMIT),
        name="moe_experts",
    )(block_e, n_used, valid, xs, w_gu, b_gu.reshape(depth, n_exp, 1, -1), w_down,
      b_down.reshape(depth, n_exp, 1, -1))


def _combine_kernel(h_ref, gate_ref, y4_ref, lng_ref, lnb_ref, out_ref):
    gates = gate_ref[...]
    y = gates[:, 0:1] * y4_ref[0]
    for k in range(1, TOP_K):
        y = y + gates[:, k:k + 1] * y4_ref[k]
    out_ref[...] = _layer_norm(ALPHA * h_ref[...] + y, lng_ref[...], lnb_ref[...])


def _combine(xt, gates, y4, ln_g, ln_b):
    t = xt.shape[0]
    rows = min(COMBINE_ROWS, t)
    tok_spec = pl.BlockSpec((rows, D_MODEL), lambda i: (i, 0))
    return pl.pallas_call(
        _combine_kernel,
        grid=(t // rows,),
        in_specs=[tok_spec,
                  pl.BlockSpec((rows, TOP_K), lambda i: (i, 0)),
                  pl.BlockSpec((TOP_K, rows, D_MODEL), lambda i: (0, i, 0)),
                  _const_spec((1, D_MODEL)), _const_spec((1, D_MODEL))],
        out_specs=tok_spec,
        out_shape=jax.ShapeDtypeStruct((t, D_MODEL), F32),
        compiler_params=pltpu.CompilerParams(dimension_semantics=("arbitrary",),
                                             vmem_limit_bytes=VMEM_LIMIT),
        name="moe_combine",
    )(xt, gates, y4, ln_g.reshape(1, -1), ln_b.reshape(1, -1))


def _moe_layer(h, layer, w_r, b_r, w_gu, b_gu, w_down, b_down, ln_g, ln_b):
    bsz, seq, d = h.shape
    t = bsz * seq
    xt = h.reshape(t, d)
    idx, gates, rank, counts = _router(xt, w_r, b_r)
    padded = ((counts + MOE_BLOCK - 1) // MOE_BLOCK) * MOE_BLOCK
    pad_ends = jnp.cumsum(padded)
    pad_starts = pad_ends - padded
    experts = jnp.arange(N_EXPERTS, dtype=jnp.int32)
    start_of = jnp.sum(jnp.where(idx[..., None] == experts, pad_starts, 0), axis=-1)
    dest = (start_of + rank).astype(jnp.int32)
    dest_km = dest.T.reshape(t * TOP_K // SC_ROWS, SC_ROWS)
    n_blocks = -(-(t * TOP_K) // MOE_BLOCK) + N_EXPERTS
    block_lo = jnp.arange(n_blocks, dtype=jnp.int32) * MOE_BLOCK
    block_e = jnp.minimum(jnp.sum(pad_ends[None, :] <= block_lo[:, None], axis=1),
                          N_EXPERTS - 1).astype(jnp.int32)
    n_used = (pad_ends[-1:] // MOE_BLOCK).astype(jnp.int32)
    valid = jnp.clip(counts[block_e] - (block_lo - pad_starts[block_e]), 0, MOE_BLOCK).astype(jnp.int32)
    xs = _dispatch(xt, dest_km, n_blocks * MOE_BLOCK)
    yb = _experts(xs, block_e, n_used, valid, layer, w_gu, b_gu, w_down, b_down)
    y4 = _gather_rows(yb, dest_km).reshape(TOP_K, t, d)
    out = _combine(xt, gates, y4, ln_g, ln_b)
    return out.reshape(bsz, seq, d)


def kernel(x, mix_w_in, sc_conv_w, ssm_conv_w, ssm_conv_b, ssm_dt_bias, ssm_a_log, ssm_d, ssm_norm_w, mix_w_out, conf_w_pw1, conf_b_pw1, conf_w_dw, conf_b_dw, conf_ln_g, conf_ln_b, conf_w_pw2, conf_b_pw2, router_w, router_b, exp_w_gu, exp_b_gu, exp_w_down, exp_b_down, ln_mix_g, ln_mix_b, ln_ffn_g, ln_ffn_b):
    h = x
    for i in range(DEPTH):
        j = i // 2
        if i % 2 == 0:
            h = _even_mixer(h, mix_w_in[j], sc_conv_w[j], ssm_conv_w[j], ssm_conv_b[j],
                            ssm_dt_bias[j], ssm_a_log[j], ssm_d[j], ssm_norm_w[j], mix_w_out[j],
                            ln_mix_g[i], ln_mix_b[i])
        else:
            h = _odd_mixer(h, conf_w_pw1[j], conf_b_pw1[j], conf_w_dw[j], conf_b_dw[j],
                           conf_ln_g[j], conf_ln_b[j], conf_w_pw2[j], conf_b_pw2[j],
                           ln_mix_g[i], ln_mix_b[i])
        h = _moe_layer(h, i, router_w[i], router_b[i], exp_w_gu, exp_b_gu, exp_w_down,
                       exp_b_down, ln_ffn_g[i], ln_ffn_b[i])
    return h
```

```python
import functools
import math

import jax
import jax.numpy as jnp
from jax import lax
from jax.experimental import pallas as pl
from jax.experimental.pallas import tpu as pltpu
from jax.experimental.pallas import tpu_sc as plsc

F32 = jnp.float32
BF16 = jnp.bfloat16

D_MODEL = 1024
DEPTH = 4
ALPHA = (2 * DEPTH) ** 0.25
LN_EPS = 1e-5
SC_KERNEL = 3
SSM_HEAD_DIM = 64
SSM_HEADS = 16
SSM_GROUPS = 2
SSM_STATE = 128
SSM_CONV = 4
SSM_CHUNK = 128
SSM_INNER = D_MODEL
GROUP_COLS = SSM_INNER // SSM_GROUPS
XBC_COLS = SSM_INNER + 2 * SSM_GROUPS * SSM_STATE
CONF_KERNEL = 31
N_EXPERTS = 32
TOP_K = 4
SWIGLU_LIMIT = 7.0
SWIGLU_ALPHA = 1.702
MOE_BLOCK = 512

LANES = 128
SUBLANES = 8
CARRY_ROWS = 8
CONF_CARRY_ROWS = 32
CONF_TILE_ROWS = 32
VMEM_LIMIT = 56 * 1024 * 1024

MIX_ROWS = 256
ROUTER_ROWS = 512
COMBINE_ROWS = 256
SC_ROWS = 64


def _layer_norm(v, g, b):
    mu = jnp.mean(v, axis=-1, keepdims=True)
    d = v - mu
    var = jnp.mean(d * d, axis=-1, keepdims=True)
    return d * lax.rsqrt(var + LN_EPS) * g + b


def _silu(v):
    return v * jax.nn.sigmoid(v)


def _softplus(v):
    return jnp.maximum(v, 0.0) + jnp.log1p(jnp.exp(-jnp.abs(v)))


def _dot(a, b):
    return jnp.dot(a, b, preferred_element_type=F32)


def _dot_exact01(m01, v):
    v1 = v.astype(BF16)
    r1 = v - v1.astype(F32)
    v2 = r1.astype(BF16)
    v3 = (r1 - v2.astype(F32)).astype(BF16)
    return _dot(m01, v1) + _dot(m01, v2) + _dot(m01, v3)


def _even_mixer_kernel(x_ref, w_sc_ref, w_z_ref, w_xbc_ref, w_dt_ref, scw_ref, ssmw_ref,
                       ssmb_ref, dtb_ref, a_ref, dexp_ref, normw_ref, wout_ref,
                       lng_ref, lnb_ref, out_ref,
                       u_buf, xbc_buf, act_buf, dtn_buf, y_buf, state_ref):
    rows = x_ref.shape[1]
    c = pl.program_id(1)

    @pl.when(c == 0)
    def _():
        u_buf[0:CARRY_ROWS, :] = jnp.zeros((CARRY_ROWS, D_MODEL), F32)
        xbc_buf[0:CARRY_ROWS, :] = jnp.zeros((CARRY_ROWS, XBC_COLS), F32)
        state_ref[...] = jnp.zeros(state_ref.shape, F32)

    x = x_ref[0]
    xb = x.astype(BF16)

    p = _dot(xb, w_sc_ref[...])
    u = p[:, D_MODEL:2 * D_MODEL] * p[:, :D_MODEL]
    u_buf[CARRY_ROWS:CARRY_ROWS + rows, :] = u
    conv = scw_ref[2:3, :] * u
    for k in range(SC_KERNEL - 1):
        shift = SC_KERNEL - 1 - k
        conv = conv + scw_ref[k:k + 1, :] * u_buf[CARRY_ROWS - shift:CARRY_ROWS - shift + rows, :]
    y_sc = p[:, 2 * D_MODEL:] * conv
    u_buf[0:CARRY_ROWS, :] = u[rows - CARRY_ROWS:, :]

    xbc = _dot(xb, w_xbc_ref[...])
    xbc_buf[CARRY_ROWS:CARRY_ROWS + rows, :] = xbc
    cv = ssmw_ref[SSM_CONV - 1:SSM_CONV, :] * xbc + ssmb_ref[...]
    for k in range(SSM_CONV - 1):
        shift = SSM_CONV - 1 - k
        cv = cv + ssmw_ref[k:k + 1, :] * xbc_buf[CARRY_ROWS - shift:CARRY_ROWS - shift + rows, :]
    act_buf[...] = _silu(cv)
    xbc_buf[0:CARRY_ROWS, :] = xbc[rows - CARRY_ROWS:, :]

    dtn_buf[...] = _softplus(_dot(xb, w_dt_ref[...]) + dtb_ref[...])

    li = lax.broadcasted_iota(jnp.int32, (SSM_CHUNK, SSM_CHUNK), 0)
    si = lax.broadcasted_iota(jnp.int32, (SSM_CHUNK, SSM_CHUNK), 1)
    causal = li >= si
    tril = jnp.where(causal, 1.0, 0.0).astype(BF16)
    low_half = si < SSM_HEAD_DIM
    low_row = low_half[0:1, :]

    for j in range(rows // SSM_CHUNK):
        r0 = j * SSM_CHUNK
        dt_c = dtn_buf[r0:r0 + SSM_CHUNK, :]
        acum = _dot_exact01(tril, dt_c * a_ref[...])
        alast = acum[SSM_CHUNK - 1:SSM_CHUNK, :]
        acum_t = acum.T
        dt_t = dt_c.T
        e_acum = jnp.exp(acum)
        w_t = (jnp.exp(alast - acum) * dt_c).T
        chunk_decay = jnp.exp(alast)
        xs_b = act_buf[r0:r0 + SSM_CHUNK, 0:SSM_INNER].astype(BF16)
        for g in range(SSM_GROUPS):
            b_g = act_buf[r0:r0 + SSM_CHUNK, SSM_INNER + g * SSM_STATE:SSM_INNER + (g + 1) * SSM_STATE]
            c_lo = SSM_INNER + (SSM_GROUPS + g) * SSM_STATE
            c_g = act_buf[r0:r0 + SSM_CHUNK, c_lo:c_lo + SSM_STATE]
            cb = lax.dot_general(c_g.astype(BF16), b_g.astype(BF16), (((1,), (1,)), ((), ())),
                                 preferred_element_type=F32)
            b_gt = b_g.T
            for pair in range(GROUP_COLS // LANES):
                col0 = g * GROUP_COLS + pair * LANES
                h0 = col0 // SSM_HEAD_DIM
                xs_p = xs_b[:, col0:col0 + LANES]
                zero_b = jnp.zeros_like(xs_p)
                xs_lo = jnp.where(low_half, xs_p, zero_b)
                xs_hi = jnp.where(low_half, zero_b, xs_p)
                st = state_ref[g, :, pair * LANES:(pair + 1) * LANES]
                st_b = st.astype(BF16)
                lhs, lhs_s = [], []
                for h in (h0, h0 + 1):
                    seg = acum[:, h:h + 1] - acum_t[h:h + 1, :]
                    decay = jnp.exp(jnp.where(causal, seg, -jnp.inf))
                    lhs.append((cb * decay * dt_t[h:h + 1, :]).astype(BF16))
                    lhs_s.append((b_gt * w_t[h:h + 1, :]).astype(BF16))
                for h in (h0, h0 + 1):
                    lhs.append((c_g * e_acum[:, h:h + 1]).astype(BF16))
                rhs = jnp.concatenate([xs_lo, xs_hi, jnp.where(low_half, st_b, zero_b),
                                       jnp.where(low_half, zero_b, st_b)], axis=0)
                y_buf[r0:r0 + SSM_CHUNK, col0:col0 + LANES] = _dot(
                    jnp.concatenate(lhs, axis=1), rhs)
                cd_p = jnp.where(low_row, chunk_decay[:, h0:h0 + 1], chunk_decay[:, h0 + 1:h0 + 2])
                state_ref[g, :, pair * LANES:(pair + 1) * LANES] = st * cd_p + _dot(
                    jnp.concatenate(lhs_s, axis=1), jnp.concatenate([xs_lo, xs_hi], axis=0))

    z = _dot(xb, w_z_ref[...])
    y = (y_buf[...] + act_buf[:, 0:SSM_INNER] * dexp_ref[...]) * _silu(z)
    parts = []
    for g in range(SSM_GROUPS):
        yg = y[:, g * GROUP_COLS:(g + 1) * GROUP_COLS]
        ms = jnp.mean(yg * yg, axis=-1, keepdims=True)
        parts.append(yg * lax.rsqrt(ms + LN_EPS) * normw_ref[:, g * GROUP_COLS:(g + 1) * GROUP_COLS])
    mix = _dot(y_sc.astype(BF16), wout_ref[0:D_MODEL, :])
    for g in range(SSM_GROUPS):
        lo = D_MODEL + g * GROUP_COLS
        mix = mix + _dot(parts[g].astype(BF16), wout_ref[lo:lo + GROUP_COLS, :])
    out_ref[0] = _layer_norm(ALPHA * x + mix, lng_ref[...], lnb_ref[...])


def _const_spec(shape):
    nd = len(shape)
    return pl.BlockSpec(shape, lambda *_: (0,) * nd, pipeline_mode=pl.Buffered(1))


def _even_mixer(h, w_in, sc_conv_w, ssm_conv_w, ssm_conv_b, dt_bias, a_log, d_skip, norm_w,
                w_out, ln_g, ln_b):
    bsz, seq, _ = h.shape
    rows = min(MIX_ROWS, seq)
    cut_z = 3 * D_MODEL
    cut_xbc = cut_z + SSM_INNER
    cut_dt = cut_xbc + XBC_COLS
    w_sc = w_in[:, :cut_z].astype(BF16)
    w_z = w_in[:, cut_z:cut_xbc].astype(BF16)
    w_xbc = w_in[:, cut_xbc:cut_dt].astype(BF16)
    pad = LANES - SSM_HEADS
    w_dt = jnp.pad(w_in[:, cut_dt:], ((0, 0), (0, pad))).astype(BF16)
    dtb = jnp.pad(dt_bias, (0, pad)).reshape(1, LANES)
    a = -jnp.exp(a_log)
    a_n = jnp.pad(a, (0, pad)).reshape(1, LANES)
    d_exp = jnp.repeat(d_skip, SSM_HEAD_DIM).reshape(1, SSM_INNER)
    consts = [w_sc, w_z, w_xbc, w_dt, sc_conv_w, ssm_conv_w, ssm_conv_b.reshape(1, -1), dtb, a_n,
              d_exp, norm_w.reshape(1, -1), w_out.astype(BF16), ln_g.reshape(1, -1),
              ln_b.reshape(1, -1)]
    x_spec = pl.BlockSpec((1, rows, D_MODEL), lambda b, c: (b, c, 0))
    return pl.pallas_call(
        _even_mixer_kernel,
        grid=(bsz, seq // rows),
        in_specs=[x_spec] + [_const_spec(v.shape) for v in consts],
        out_specs=x_spec,
        out_shape=jax.ShapeDtypeStruct(h.shape, F32),
        scratch_shapes=[
            pltpu.VMEM((CARRY_ROWS + rows, D_MODEL), F32),
            pltpu.VMEM((CARRY_ROWS + rows, XBC_COLS), F32),
            pltpu.VMEM((rows, XBC_COLS), F32),
            pltpu.VMEM((rows, LANES), F32),
            pltpu.VMEM((rows, SSM_INNER), F32),
            pltpu.VMEM((SSM_GROUPS, SSM_STATE, GROUP_COLS), F32),
        ],
        compiler_params=pltpu.CompilerParams(
            dimension_semantics=("arbitrary", "arbitrary"), vmem_limit_bytes=VMEM_LIMIT),
        name="even_mixer",
    )(h, *consts)


def _odd_mixer_kernel(x_ref, w1_ref, b1_ref, wdw_ref, bdw_ref, g_ref, b_ref, w2_ref, b2_ref,
                      lng_ref, lnb_ref, out_ref, sh_buf, conv_buf):
    rows = x_ref.shape[1]
    c = pl.program_id(1)

    @pl.when(c == 0)
    def _():
        sh_buf[0, 0:CONF_CARRY_ROWS, :] = jnp.zeros((CONF_CARRY_ROWS, D_MODEL), F32)

    x = x_ref[0]
    u = _dot(x.astype(BF16), w1_ref[...]) + b1_ref[...]
    u = u[:, :D_MODEL] * jax.nn.sigmoid(u[:, D_MODEL:])
    sh_buf[0, CONF_CARRY_ROWS:CONF_CARRY_ROWS + rows, :] = u
    span = rows + CONF_CARRY_ROWS - SUBLANES
    for r in range(1, SUBLANES):
        sh_buf[r, 0:span, :] = sh_buf[0, r:r + span, :]

    def conv_tile(i, carry):
        base = pl.multiple_of(i * CONF_TILE_ROWS, CONF_TILE_ROWS)
        groups = CONF_TILE_ROWS // SUBLANES
        acc = jnp.broadcast_to(bdw_ref[...][None], (groups, SUBLANES, D_MODEL))
        for k in range(CONF_KERNEL):
            lo = CONF_CARRY_ROWS - (CONF_KERNEL - 1 - k)
            tap = sh_buf[lo % SUBLANES, pl.ds(base + (lo // SUBLANES) * SUBLANES, CONF_TILE_ROWS), :]
            acc = acc + wdw_ref[k][None] * tap.reshape(groups, SUBLANES, D_MODEL)
        conv_buf[pl.ds(base, CONF_TILE_ROWS), :] = acc.reshape(CONF_TILE_ROWS, D_MODEL)
        return carry

    lax.fori_loop(0, rows // CONF_TILE_ROWS, conv_tile, 0)
    sh_buf[0, 0:CONF_CARRY_ROWS, :] = u[rows - CONF_CARRY_ROWS:, :]
    v = _silu(_layer_norm(conv_buf[...], g_ref[...], b_ref[...]))
    mix = _dot(v.astype(BF16), w2_ref[...]) + b2_ref[...]
    out_ref[0] = _layer_norm(ALPHA * x + mix, lng_ref[...], lnb_ref[...])


def _odd_mixer(h, w_pw1, b_pw1, w_dw, b_dw, ln_g_c, ln_b_c, w_pw2, b_pw2, ln_g, ln_b):
    bsz, seq, _ = h.shape
    rows = min(MIX_ROWS, seq)
    w_dw_rows = jnp.broadcast_to(w_dw[:, None, :], (CONF_KERNEL, SUBLANES, D_MODEL))
    b_dw_rows = jnp.broadcast_to(b_dw[None, :], (SUBLANES, D_MODEL))
    consts = [w_pw1.astype(BF16), b_pw1.reshape(1, -1), w_dw_rows, b_dw_rows,
              ln_g_c.reshape(1, -1), ln_b_c.reshape(1, -1), w_pw2.astype(BF16),
              b_pw2.reshape(1, -1), ln_g.reshape(1, -1), ln_b.reshape(1, -1)]
    x_spec = pl.BlockSpec((1, rows, D_MODEL), lambda b, c: (b, c, 0))
    return pl.pallas_call(
        _odd_mixer_kernel,
        grid=(bsz, seq // rows),
        in_specs=[x_spec] + [_const_spec(v.shape) for v in consts],
        out_specs=x_spec,
        out_shape=jax.ShapeDtypeStruct(h.shape, F32),
        scratch_shapes=[pltpu.VMEM((SUBLANES, CONF_CARRY_ROWS + rows, D_MODEL), F32),
                        pltpu.VMEM((rows, D_MODEL), F32)],
        compiler_params=pltpu.CompilerParams(
            dimension_semantics=("arbitrary", "arbitrary"), vmem_limit_bytes=VMEM_LIMIT),
        name="odd_mixer",
    )(h, *consts)


def _pack_halves(v):
    n = v.shape[1] // 2
    lo = lax.bitcast_convert_type(v[:, :n].astype(BF16).astype(F32), jnp.uint32)
    hi = lax.bitcast_convert_type(v[:, n:].astype(BF16).astype(F32), jnp.uint32)
    return (lo >> 16) | (hi & jnp.uint32(0xFFFF0000))


def _unpack_halves(w):
    lo = lax.bitcast_convert_type(w << 16, F32)
    hi = lax.bitcast_convert_type(w & jnp.uint32(0xFFFF0000), F32)
    return lo, hi


def _router_kernel(x_ref, w_hi_ref, w_lo_ref, b_ref, idx_ref, gate_ref, rank_ref, cnt_ref, xp_ref,
                   carry_ref):
    rows = x_ref.shape[0]
    i = pl.program_id(0)

    @pl.when(i == 0)
    def _():
        carry_ref[...] = jnp.zeros(carry_ref.shape, F32)

    x = x_ref[...]
    x1 = x.astype(BF16)
    x2 = (x - x1.astype(F32)).astype(BF16)
    logits = (_dot(x1, w_hi_ref[...]) + (_dot(x1, w_lo_ref[...]) + _dot(x2, w_hi_ref[...]))
              + b_ref[...])
    xp_ref[...] = _pack_halves(x)
    lane = lax.broadcasted_iota(jnp.int32, (rows, LANES), 1)
    lane_f = lane.astype(F32)
    work = logits
    vals, sels, idxs = [], [], []
    for _ in range(TOP_K):
        m = jnp.max(work, axis=-1, keepdims=True)
        first = jnp.min(jnp.where(work == m, lane_f, float(LANES)), axis=-1, keepdims=True)
        sel = lane_f == first
        work = jnp.where(sel, -jnp.inf, work)
        vals.append(m)
        sels.append(sel)
        idxs.append(first)
    exps = [jnp.exp(v - vals[0]) for v in vals]
    inv = 1.0 / (exps[0] + exps[1] + exps[2] + exps[3])
    onehot = jnp.zeros((rows, LANES), F32)
    for sel in sels:
        onehot = onehot + jnp.where(sel, 1.0, 0.0)
    ri = lax.broadcasted_iota(jnp.int32, (rows, rows), 0)
    ci = lax.broadcasted_iota(jnp.int32, (rows, rows), 1)
    strict = jnp.where(ri > ci, 1.0, 0.0).astype(BF16)
    before = _dot(strict, onehot.astype(BF16)) + carry_ref[0:1, :]
    idx_o = jnp.zeros((rows, LANES), F32)
    gate_o = jnp.zeros((rows, LANES), F32)
    rank_o = jnp.zeros((rows, LANES), F32)
    for k in range(TOP_K):
        rank_k = jnp.sum(jnp.where(sels[k], before, 0.0), axis=-1, keepdims=True)
        here = lane == k
        idx_o = jnp.where(here, idxs[k], idx_o)
        gate_o = jnp.where(here, exps[k] * inv, gate_o)
        rank_o = jnp.where(here, rank_k, rank_o)
    gate_ref[...] = gate_o
    idx_ref[...] = idx_o.T[0:SUBLANES, :].astype(jnp.int32)
    rank_ref[...] = rank_o.T[0:SUBLANES, :].astype(jnp.int32)
    total = carry_ref[0:1, :] + jnp.sum(onehot, axis=0, keepdims=True)
    carry_ref[...] = jnp.broadcast_to(total, carry_ref.shape)
    cnt_ref[...] = jnp.broadcast_to(total, cnt_ref.shape).astype(jnp.int32)


def _router(xt, w_r, b_r):
    t = xt.shape[0]
    rows = min(ROUTER_ROWS, t)
    pad = LANES - N_EXPERTS
    w = jnp.pad(w_r, ((0, 0), (0, pad)))
    w_hi = w.astype(BF16)
    w_lo = (w - w_hi.astype(F32)).astype(BF16)
    b = jnp.pad(b_r, (0, pad), constant_values=-1e30).reshape(1, LANES)
    km_spec = pl.BlockSpec((SUBLANES, rows), lambda i: (0, i))
    idx, gate, rank, cnt, xp = pl.pallas_call(
        _router_kernel,
        grid=(t // rows,),
        in_specs=[pl.BlockSpec((rows, D_MODEL), lambda i: (i, 0)), _const_spec(w.shape),
                  _const_spec(w.shape), _const_spec(b.shape)],
        out_specs=[km_spec, pl.BlockSpec((rows, LANES), lambda i: (i, 0)), km_spec,
                   pl.BlockSpec((SUBLANES, LANES), lambda i: (0, 0)),
                   pl.BlockSpec((rows, D_MODEL // 2), lambda i: (i, 0))],
        out_shape=[jax.ShapeDtypeStruct((SUBLANES, t), jnp.int32),
                   jax.ShapeDtypeStruct((t, LANES), F32),
                   jax.ShapeDtypeStruct((SUBLANES, t), jnp.int32),
                   jax.ShapeDtypeStruct((SUBLANES, LANES), jnp.int32),
                   jax.ShapeDtypeStruct((t, D_MODEL // 2), jnp.uint32)],
        scratch_shapes=[pltpu.VMEM((SUBLANES, LANES), F32)],
        compiler_params=pltpu.CompilerParams(dimension_semantics=("arbitrary",),
                                             vmem_limit_bytes=VMEM_LIMIT),
        name="moe_router",
    )(xt, w_hi, w_lo, b)
    return idx[:TOP_K], gate, rank[:TOP_K], cnt[0, :N_EXPERTS], xp


def _sc_workers():
    mesh = plsc.VectorSubcoreMesh(core_axis_name="c", subcore_axis_name="s")
    return mesh, mesh.num_cores, mesh.num_cores * mesh.num_subcores


def _dispatch(xt, dest_km, n_rows):
    t = xt.shape[0]
    mesh, n_cores, n_workers = _sc_workers()
    n_ch = t // n_workers // SC_ROWS
    assert n_ch * SC_ROWS * n_workers == t and n_ch % 2 == 0

    def body(x_hbm, dest_hbm, xs_hbm, idx_v, rows_v, load_sem, scat_sem):
        wid = lax.axis_index("s") * n_cores + lax.axis_index("c")
        base = wid * (n_ch * SC_ROWS)
        for k in range(TOP_K):
            pltpu.sync_copy(dest_hbm.at[pl.ds(k * (t // SC_ROWS) + wid * n_ch, n_ch)], idx_v.at[k])

        def load(j, b):
            off = pl.multiple_of(base + j * SC_ROWS, SC_ROWS)
            return pltpu.make_async_copy(x_hbm.at[pl.ds(off, SC_ROWS)], rows_v.at[b], load_sem.at[b])

        def scatter(j, b, k):
            return pltpu.make_async_copy(rows_v.at[b], xs_hbm.at[idx_v.at[k, j]], scat_sem.at[b])

        load(0, 0).start()

        @pl.loop(0, n_ch, step=2)
        def _(j0):
            for b in range(2):
                j = j0 + b
                load(j, b).wait()

                @pl.when(j >= 1)
                def _():
                    for k in range(TOP_K):
                        scatter(j - 1, 1 - b, k).wait()

                @pl.when(j + 1 < n_ch)
                def _():
                    load(j + 1, 1 - b).start()

                for k in range(TOP_K):
                    scatter(j, b, k).start()

        for k in range(TOP_K):
            scatter(n_ch - 1, 1, k).wait()

    return pl.kernel(
        body, mesh=mesh,
        out_type=jax.ShapeDtypeStruct((n_rows, xt.shape[1]), xt.dtype),
        scratch_types=[pltpu.VMEM((TOP_K, n_ch, SC_ROWS), jnp.int32),
                       pltpu.VMEM((2, SC_ROWS, xt.shape[1]), xt.dtype),
                       pltpu.SemaphoreType.DMA((2,)), pltpu.SemaphoreType.DMA((2,))],
    )(xt, dest_km)


def _gather_rows(yb, dest_km):
    n_idx = dest_km.shape[0] * SC_ROWS
    mesh, n_cores, n_workers = _sc_workers()
    n_ch = n_idx // n_workers // SC_ROWS
    assert n_ch * SC_ROWS * n_workers == n_idx and n_ch % 2 == 0

    def body(yb_hbm, dest_hbm, out_hbm, idx_v, rows_v, gat_sem, store_sem):
        wid = lax.axis_index("s") * n_cores + lax.axis_index("c")
        base = wid * (n_ch * SC_ROWS)
        pltpu.sync_copy(dest_hbm.at[pl.ds(wid * n_ch, n_ch)], idx_v)

        def gather(j, b):
            return pltpu.make_async_copy(yb_hbm.at[idx_v.at[j]], rows_v.at[b], gat_sem.at[b])

        def store(j, b):
            off = pl.multiple_of(base + j * SC_ROWS, SC_ROWS)
            return pltpu.make_async_copy(rows_v.at[b], out_hbm.at[pl.ds(off, SC_ROWS)], store_sem.at[b])

        gather(0, 0).start()

        @pl.loop(0, n_ch, step=2)
        def _(j0):
            for b in range(2):
                j = j0 + b
                gather(j, b).wait()

                @pl.when(j >= 1)
                def _():
                    store(j - 1, 1 - b).wait()

                @pl.when(j + 1 < n_ch)
                def _():
                    gather(j + 1, 1 - b).start()

                store(j, b).start()

        store(n_ch - 1, 1).wait()

    return pl.kernel(
        body, mesh=mesh,
        out_type=jax.ShapeDtypeStruct((n_idx, yb.shape[1]), yb.dtype),
        scratch_types=[pltpu.VMEM((n_ch, SC_ROWS), jnp.int32),
                       pltpu.VMEM((2, SC_ROWS, yb.shape[1]), yb.dtype),
                       pltpu.SemaphoreType.DMA((2,)), pltpu.SemaphoreType.DMA((2,))],
    )(yb, dest_km)


def _expert_kernel(be_ref, nused_ref, valid_ref, x_ref, wgu_ref, bgu_ref, wd_ref, bd_ref, out_ref):
    del be_ref
    i = pl.program_id(0)

    @pl.when(i < nused_ref[0])
    def _():
        row = lax.broadcasted_iota(jnp.int32, x_ref.shape, 0)
        x_lo, x_hi = _unpack_halves(jnp.where(row < valid_ref[i], x_ref[...], jnp.uint32(0)))
        half = x_ref.shape[1]
        gu = (_dot(x_lo.astype(BF16), wgu_ref[0, 0, 0:half, :].astype(BF16))
              + _dot(x_hi.astype(BF16), wgu_ref[0, 0, half:, :].astype(BF16)) + bgu_ref[0, 0])
        d_exp = wd_ref.shape[2]
        gate = jnp.minimum(gu[:, :d_exp], SWIGLU_LIMIT)
        up = jnp.clip(gu[:, d_exp:], -SWIGLU_LIMIT, SWIGLU_LIMIT)
        act = (up + 1.0) * (gate * jax.nn.sigmoid(SWIGLU_ALPHA * gate))
        out_ref[...] = _pack_halves(_dot(act.astype(BF16), wd_ref[0, 0].astype(BF16)) + bd_ref[0, 0])

    @pl.when(i >= nused_ref[0])
    def _():
        out_ref[...] = jnp.zeros(out_ref.shape, jnp.uint32)


def _experts(xs, block_e, n_used, valid, layer, w_gu, b_gu, w_down, b_down):
    n_rows = xs.shape[0]
    n_blocks = n_rows // MOE_BLOCK
    depth, n_exp, d_exp = w_down.shape[:3]
    grid_spec = pltpu.PrefetchScalarGridSpec(
        num_scalar_prefetch=3,
        grid=(n_blocks,),
        in_specs=[
            pl.BlockSpec((MOE_BLOCK, D_MODEL // 2), lambda i, be, nu, va: (jnp.minimum(i, nu[0] - 1), 0)),
            pl.BlockSpec((1, 1, D_MODEL, 2 * d_exp), lambda i, be, nu, va: (layer, be[i], 0, 0)),
            pl.BlockSpec((1, 1, 1, 2 * d_exp), lambda i, be, nu, va: (layer, be[i], 0, 0)),
            pl.BlockSpec((1, 1, d_exp, D_MODEL), lambda i, be, nu, va: (layer, be[i], 0, 0)),
            pl.BlockSpec((1, 1, 1, D_MODEL), lambda i, be, nu, va: (layer, be[i], 0, 0)),
        ],
        out_specs=pl.BlockSpec((MOE_BLOCK, D_MODEL // 2), lambda i, be, nu, va: (i, 0)),
    )
    return pl.pallas_call(
        _expert_kernel,
        grid_spec=grid_spec,
        out_shape=jax.ShapeDtypeStruct((n_rows, D_MODEL // 2), jnp.uint32),
        compiler_params=pltpu.CompilerParams(dimension_semantics=("arbitrary",),
                                             vmem_limit_bytes=VMEM_LIMIT),
        name="moe_experts",
    )(block_e, n_used, valid, xs, w_gu, b_gu.reshape(depth, n_exp, 1, -1), w_down,
      b_down.reshape(depth, n_exp, 1, -1))


def _combine_kernel(h_ref, gate_ref, y4_ref, lng_ref, lnb_ref, out_ref):
    gates = gate_ref[...]
    y_lo, y_hi = None, None
    for k in range(TOP_K):
        lo, hi = _unpack_halves(y4_ref[k])
        g = gates[:, k:k + 1]
        y_lo = g * lo if y_lo is None else y_lo + g * lo
        y_hi = g * hi if y_hi is None else y_hi + g * hi
    y = jnp.concatenate([y_lo, y_hi], axis=1)
    out_ref[...] = _layer_norm(ALPHA * h_ref[...] + y, lng_ref[...], lnb_ref[...])


def _combine(xt, gates, y4, ln_g, ln_b):
    t = xt.shape[0]
    rows = min(COMBINE_ROWS, t)
    tok_spec = pl.BlockSpec((rows, D_MODEL), lambda i: (i, 0))
    return pl.pallas_call(
        _combine_kernel,
        grid=(t // rows,),
        in_specs=[tok_spec,
                  pl.BlockSpec((rows, LANES), lambda i: (i, 0)),
                  pl.BlockSpec((TOP_K, rows, D_MODEL // 2), lambda i: (0, i, 0)),
                  _const_spec((1, D_MODEL)), _const_spec((1, D_MODEL))],
        out_specs=tok_spec,
        out_shape=jax.ShapeDtypeStruct((t, D_MODEL), F32),
        compiler_params=pltpu.CompilerParams(dimension_semantics=("arbitrary",),
                                             vmem_limit_bytes=VMEM_LIMIT),
        name="moe_combine",
    )(xt, gates, y4, ln_g.reshape(1, -1), ln_b.reshape(1, -1))


def _moe_layer(h, layer, w_r, b_r, w_gu, b_gu, w_down, b_down, ln_g, ln_b):
    bsz, seq, d = h.shape
    t = bsz * seq
    xt = h.reshape(t, d)
    idx, gates, rank, counts, xp = _router(xt, w_r, b_r)
    padded = ((counts + MOE_BLOCK - 1) // MOE_BLOCK) * MOE_BLOCK
    pad_ends = jnp.cumsum(padded)
    pad_starts = pad_ends - padded
    start_of = jnp.zeros_like(idx)
    for e in range(N_EXPERTS):
        start_of = jnp.where(idx == e, pad_starts[e], start_of)
    dest_km = (start_of + rank).reshape(t * TOP_K // SC_ROWS, SC_ROWS)
    n_blocks = -(-(t * TOP_K) // MOE_BLOCK) + N_EXPERTS
    block_lo = jnp.arange(n_blocks, dtype=jnp.int32) * MOE_BLOCK
    block_e = jnp.minimum(jnp.sum(pad_ends[None, :] <= block_lo[:, None], axis=1),
                          N_EXPERTS - 1).astype(jnp.int32)
    n_used = (pad_ends[-1:] // MOE_BLOCK).astype(jnp.int32)
    valid = jnp.clip(counts[block_e] - (block_lo - pad_starts[block_e]), 0, MOE_BLOCK).astype(jnp.int32)
    xs = _dispatch(xp, dest_km, n_blocks * MOE_BLOCK)
    yb = _experts(xs, block_e, n_used, valid, layer, w_gu, b_gu, w_down, b_down)
    y4 = _gather_rows(yb, dest_km).reshape(TOP_K, t, d // 2)
    out = _combine(xt, gates, y4, ln_g, ln_b)
    return out.reshape(bsz, seq, d)


def kernel(x, mix_w_in, sc_conv_w, ssm_conv_w, ssm_conv_b, ssm_dt_bias, ssm_a_log, ssm_d, ssm_norm_w, mix_w_out, conf_w_pw1, conf_b_pw1, conf_w_dw, conf_b_dw, conf_ln_g, conf_ln_b, conf_w_pw2, conf_b_pw2, router_w, router_b, exp_w_gu, exp_b_gu, exp_w_down, exp_b_down, ln_mix_g, ln_mix_b, ln_ffn_g, ln_ffn_b):
    h = x
    for i in range(DEPTH):
        j = i // 2
        if i % 2 == 0:
            h = _even_mixer(h, mix_w_in[j], sc_conv_w[j], ssm_conv_w[j], ssm_conv_b[j],
                            ssm_dt_bias[j], ssm_a_log[j], ssm_d[j], ssm_norm_w[j], mix_w_out[j],
                            ln_mix_g[i], ln_mix_b[i])
        else:
            h = _odd_mixer(h, conf_w_pw1[j], conf_b_pw1[j], conf_w_dw[j], conf_b_dw[j],
                           conf_ln_g[j], conf_ln_b[j], conf_w_pw2[j], conf_b_pw2[j],
                           ln_mix_g[i], ln_mix_b[i])
        h = _moe_layer(h, i, router_w[i], router_b[i], exp_w_gu, exp_b_gu, exp_w_down,
                       exp_b_down, ln_ffn_g[i], ln_ffn_b[i])
    return h
```

```python
import functools
import math

import jax
import jax.numpy as jnp
from jax import lax
from jax.experimental import pallas as pl
from jax.experimental.pallas import tpu as pltpu
from jax.experimental.pallas import tpu_sc as plsc

F32 = jnp.float32
BF16 = jnp.bfloat16

D_MODEL = 1024
DEPTH = 4
ALPHA = (2 * DEPTH) ** 0.25
LN_EPS = 1e-5
SC_KERNEL = 3
SSM_HEAD_DIM = 64
SSM_HEADS = 16
SSM_GROUPS = 2
SSM_STATE = 128
SSM_CONV = 4
SSM_CHUNK = 128
SSM_INNER = D_MODEL
GROUP_COLS = SSM_INNER // SSM_GROUPS
XBC_COLS = SSM_INNER + 2 * SSM_GROUPS * SSM_STATE
CONF_KERNEL = 31
N_EXPERTS = 32
TOP_K = 4
SWIGLU_LIMIT = 7.0
SWIGLU_ALPHA = 1.702
MOE_BLOCK = 512

LANES = 128
SUBLANES = 8
CARRY_ROWS = 8
CONF_CARRY_ROWS = 32
CONF_TILE_ROWS = 32
VMEM_LIMIT = 56 * 1024 * 1024

MIX_ROWS = 256
ROUTER_ROWS = 512
COMBINE_ROWS = 256
SC_ROWS = 64


def _layer_norm(v, g, b):
    mu = jnp.mean(v, axis=-1, keepdims=True)
    d = v - mu
    var = jnp.mean(d * d, axis=-1, keepdims=True)
    return d * lax.rsqrt(var + LN_EPS) * g + b


def _silu(v):
    return v * jax.nn.sigmoid(v)


def _softplus(v):
    return jnp.maximum(v, 0.0) + jnp.log1p(jnp.exp(-jnp.abs(v)))


def _dot(a, b):
    return jnp.dot(a, b, preferred_element_type=F32)


def _dot_exact01(m01, v):
    v1 = v.astype(BF16)
    r1 = v - v1.astype(F32)
    v2 = r1.astype(BF16)
    v3 = (r1 - v2.astype(F32)).astype(BF16)
    return _dot(m01, v1) + _dot(m01, v2) + _dot(m01, v3)


def _even_mixer_kernel(x_ref, w_sc_ref, w_z_ref, w_xbc_ref, w_dt_ref, scw_ref, ssmw_ref,
                       ssmb_ref, dtb_ref, a_ref, dexp_ref, normw_ref, wout_ref,
                       lng_ref, lnb_ref, out_ref,
                       u_buf, xbc_buf, act_buf, dtn_buf, y_buf, state_ref):
    rows = x_ref.shape[1]
    c = pl.program_id(1)

    @pl.when(c == 0)
    def _():
        u_buf[0:CARRY_ROWS, :] = jnp.zeros((CARRY_ROWS, D_MODEL), F32)
        xbc_buf[0:CARRY_ROWS, :] = jnp.zeros((CARRY_ROWS, XBC_COLS), F32)
        state_ref[...] = jnp.zeros(state_ref.shape, F32)

    x = x_ref[0]
    xb = x.astype(BF16)

    p = _dot(xb, w_sc_ref[...])
    u = p[:, D_MODEL:2 * D_MODEL] * p[:, :D_MODEL]
    u_buf[CARRY_ROWS:CARRY_ROWS + rows, :] = u
    conv = scw_ref[2:3, :] * u
    for k in range(SC_KERNEL - 1):
        shift = SC_KERNEL - 1 - k
        conv = conv + scw_ref[k:k + 1, :] * u_buf[CARRY_ROWS - shift:CARRY_ROWS - shift + rows, :]
    y_sc = p[:, 2 * D_MODEL:] * conv
    u_buf[0:CARRY_ROWS, :] = u[rows - CARRY_ROWS:, :]

    xbc = _dot(xb, w_xbc_ref[...])
    xbc_buf[CARRY_ROWS:CARRY_ROWS + rows, :] = xbc
    cv = ssmw_ref[SSM_CONV - 1:SSM_CONV, :] * xbc + ssmb_ref[...]
    for k in range(SSM_CONV - 1):
        shift = SSM_CONV - 1 - k
        cv = cv + ssmw_ref[k:k + 1, :] * xbc_buf[CARRY_ROWS - shift:CARRY_ROWS - shift + rows, :]
    act_buf[...] = _silu(cv)
    xbc_buf[0:CARRY_ROWS, :] = xbc[rows - CARRY_ROWS:, :]

    dtn_buf[...] = _softplus(_dot(xb, w_dt_ref[...]) + dtb_ref[...])

    li = lax.broadcasted_iota(jnp.int32, (SSM_CHUNK, SSM_CHUNK), 0)
    si = lax.broadcasted_iota(jnp.int32, (SSM_CHUNK, SSM_CHUNK), 1)
    causal = li >= si
    tril = jnp.where(causal, 1.0, 0.0).astype(BF16)
    low_half = si < SSM_HEAD_DIM
    low_row = low_half[0:1, :]

    for j in range(rows // SSM_CHUNK):
        r0 = j * SSM_CHUNK
        dt_c = dtn_buf[r0:r0 + SSM_CHUNK, :]
        acum = _dot_exact01(tril, dt_c * a_ref[...])
        alast = acum[SSM_CHUNK - 1:SSM_CHUNK, :]
        acum_t = acum.T
        dt_t = dt_c.T
        e_acum = jnp.exp(acum)
        w_t = (jnp.exp(alast - acum) * dt_c).T
        chunk_decay = jnp.exp(alast)
        xs_b = act_buf[r0:r0 + SSM_CHUNK, 0:SSM_INNER].astype(BF16)
        for g in range(SSM_GROUPS):
            b_g = act_buf[r0:r0 + SSM_CHUNK, SSM_INNER + g * SSM_STATE:SSM_INNER + (g + 1) * SSM_STATE]
            c_lo = SSM_INNER + (SSM_GROUPS + g) * SSM_STATE
            c_g = act_buf[r0:r0 + SSM_CHUNK, c_lo:c_lo + SSM_STATE]
            cb = lax.dot_general(c_g.astype(BF16), b_g.astype(BF16), (((1,), (1,)), ((), ())),
                                 preferred_element_type=F32)
            b_gt = b_g.T
            for pair in range(GROUP_COLS // LANES):
                col0 = g * GROUP_COLS + pair * LANES
                h0 = col0 // SSM_HEAD_DIM
                xs_p = xs_b[:, col0:col0 + LANES]
                zero_b = jnp.zeros_like(xs_p)
                xs_lo = jnp.where(low_half, xs_p, zero_b)
                xs_hi = jnp.where(low_half, zero_b, xs_p)
                st = state_ref[g, :, pair * LANES:(pair + 1) * LANES]
                st_b = st.astype(BF16)
                lhs, lhs_s = [], []
                for h in (h0, h0 + 1):
                    seg = acum[:, h:h + 1] - acum_t[h:h + 1, :]
                    decay = jnp.exp(jnp.where(causal, seg, -jnp.inf))
                    lhs.append((cb * decay * dt_t[h:h + 1, :]).astype(BF16))
                    lhs_s.append((b_gt * w_t[h:h + 1, :]).astype(BF16))
                for h in (h0, h0 + 1):
                    lhs.append((c_g * e_acum[:, h:h + 1]).astype(BF16))
                rhs = jnp.concatenate([xs_lo, xs_hi, jnp.where(low_half, st_b, zero_b),
                                       jnp.where(low_half, zero_b, st_b)], axis=0)
                y_buf[r0:r0 + SSM_CHUNK, col0:col0 + LANES] = _dot(
                    jnp.concatenate(lhs, axis=1), rhs)
                cd_p = jnp.where(low_row, chunk_decay[:, h0:h0 + 1], chunk_decay[:, h0 + 1:h0 + 2])
                state_ref[g, :, pair * LANES:(pair + 1) * LANES] = st * cd_p + _dot(
                    jnp.concatenate(lhs_s, axis=1), jnp.concatenate([xs_lo, xs_hi], axis=0))

    z = _dot(xb, w_z_ref[...])
    y = (y_buf[...] + act_buf[:, 0:SSM_INNER] * dexp_ref[...]) * _silu(z)
    parts = []
    for g in range(SSM_GROUPS):
        yg = y[:, g * GROUP_COLS:(g + 1) * GROUP_COLS]
        ms = jnp.mean(yg * yg, axis=-1, keepdims=True)
        parts.append(yg * lax.rsqrt(ms + LN_EPS) * normw_ref[:, g * GROUP_COLS:(g + 1) * GROUP_COLS])
    mix = _dot(y_sc.astype(BF16), wout_ref[0:D_MODEL, :])
    for g in range(SSM_GROUPS):
        lo = D_MODEL + g * GROUP_COLS
        mix = mix + _dot(parts[g].astype(BF16), wout_ref[lo:lo + GROUP_COLS, :])
    out_ref[0] = _layer_norm(ALPHA * x + mix, lng_ref[...], lnb_ref[...])


def _const_spec(shape):
    nd = len(shape)
    return pl.BlockSpec(shape, lambda *_: (0,) * nd, pipeline_mode=pl.Buffered(1))


def _even_mixer(h, w_in, sc_conv_w, ssm_conv_w, ssm_conv_b, dt_bias, a_log, d_skip, norm_w,
                w_out, ln_g, ln_b):
    bsz, seq, _ = h.shape
    rows = min(MIX_ROWS, seq)
    cut_z = 3 * D_MODEL
    cut_xbc = cut_z + SSM_INNER
    cut_dt = cut_xbc + XBC_COLS
    w_sc = w_in[:, :cut_z].astype(BF16)
    w_z = w_in[:, cut_z:cut_xbc].astype(BF16)
    w_xbc = w_in[:, cut_xbc:cut_dt].astype(BF16)
    pad = LANES - SSM_HEADS
    w_dt = jnp.pad(w_in[:, cut_dt:], ((0, 0), (0, pad))).astype(BF16)
    dtb = jnp.pad(dt_bias, (0, pad)).reshape(1, LANES)
    a = -jnp.exp(a_log)
    a_n = jnp.pad(a, (0, pad)).reshape(1, LANES)
    d_exp = jnp.repeat(d_skip, SSM_HEAD_DIM).reshape(1, SSM_INNER)
    consts = [w_sc, w_z, w_xbc, w_dt, sc_conv_w, ssm_conv_w, ssm_conv_b.reshape(1, -1), dtb, a_n,
              d_exp, norm_w.reshape(1, -1), w_out.astype(BF16), ln_g.reshape(1, -1),
              ln_b.reshape(1, -1)]
    x_spec = pl.BlockSpec((1, rows, D_MODEL), lambda b, c: (b, c, 0))
    return pl.pallas_call(
        _even_mixer_kernel,
        grid=(bsz, seq // rows),
        in_specs=[x_spec] + [_const_spec(v.shape) for v in consts],
        out_specs=x_spec,
        out_shape=jax.ShapeDtypeStruct(h.shape, F32),
        scratch_shapes=[
            pltpu.VMEM((CARRY_ROWS + rows, D_MODEL), F32),
            pltpu.VMEM((CARRY_ROWS + rows, XBC_COLS), F32),
            pltpu.VMEM((rows, XBC_COLS), F32),
            pltpu.VMEM((rows, LANES), F32),
            pltpu.VMEM((rows, SSM_INNER), F32),
            pltpu.VMEM((SSM_GROUPS, SSM_STATE, GROUP_COLS), F32),
        ],
        compiler_params=pltpu.CompilerParams(
            dimension_semantics=("arbitrary", "arbitrary"), vmem_limit_bytes=VMEM_LIMIT),
        name="even_mixer",
    )(h, *consts)


def _odd_mixer_kernel(x_ref, w1_ref, b1_ref, wdw_ref, bdw_ref, g_ref, b_ref, w2_ref, b2_ref,
                      lng_ref, lnb_ref, out_ref, sh_buf, conv_buf):
    rows = x_ref.shape[1]
    c = pl.program_id(1)

    @pl.when(c == 0)
    def _():
        sh_buf[0, 0:CONF_CARRY_ROWS, :] = jnp.zeros((CONF_CARRY_ROWS, D_MODEL), F32)

    x = x_ref[0]
    u = _dot(x.astype(BF16), w1_ref[...]) + b1_ref[...]
    u = u[:, :D_MODEL] * jax.nn.sigmoid(u[:, D_MODEL:])
    sh_buf[0, CONF_CARRY_ROWS:CONF_CARRY_ROWS + rows, :] = u
    span = rows + CONF_CARRY_ROWS - SUBLANES
    for r in range(1, SUBLANES):
        sh_buf[r, 0:span, :] = sh_buf[0, r:r + span, :]

    def conv_tile(i, carry):
        base = pl.multiple_of(i * CONF_TILE_ROWS, CONF_TILE_ROWS)
        groups = CONF_TILE_ROWS // SUBLANES
        acc = [bdw_ref[...] for _ in range(groups)]
        for k in range(CONF_KERNEL):
            lo = CONF_CARRY_ROWS - (CONF_KERNEL - 1 - k)
            w_k = wdw_ref[k]
            for g in range(groups):
                row0 = base + (lo // SUBLANES + g) * SUBLANES
                acc[g] = acc[g] + w_k * sh_buf[lo % SUBLANES, pl.ds(row0, SUBLANES), :]
        for g in range(groups):
            conv_buf[pl.ds(base + g * SUBLANES, SUBLANES), :] = acc[g]
        return carry

    lax.fori_loop(0, rows // CONF_TILE_ROWS, conv_tile, 0)
    sh_buf[0, 0:CONF_CARRY_ROWS, :] = u[rows - CONF_CARRY_ROWS:, :]
    v = _silu(_layer_norm(conv_buf[...], g_ref[...], b_ref[...]))
    mix = _dot(v.astype(BF16), w2_ref[...]) + b2_ref[...]
    out_ref[0] = _layer_norm(ALPHA * x + mix, lng_ref[...], lnb_ref[...])


def _odd_mixer(h, w_pw1, b_pw1, w_dw, b_dw, ln_g_c, ln_b_c, w_pw2, b_pw2, ln_g, ln_b):
    bsz, seq, _ = h.shape
    rows = min(MIX_ROWS, seq)
    w_dw_rows = jnp.broadcast_to(w_dw[:, None, :], (CONF_KERNEL, SUBLANES, D_MODEL))
    b_dw_rows = jnp.broadcast_to(b_dw[None, :], (SUBLANES, D_MODEL))
    consts = [w_pw1.astype(BF16), b_pw1.reshape(1, -1), w_dw_rows, b_dw_rows,
              ln_g_c.reshape(1, -1), ln_b_c.reshape(1, -1), w_pw2.astype(BF16),
              b_pw2.reshape(1, -1), ln_g.reshape(1, -1), ln_b.reshape(1, -1)]
    x_spec = pl.BlockSpec((1, rows, D_MODEL), lambda b, c: (b, c, 0))
    return pl.pallas_call(
        _odd_mixer_kernel,
        grid=(bsz, seq // rows),
        in_specs=[x_spec] + [_const_spec(v.shape) for v in consts],
        out_specs=x_spec,
        out_shape=jax.ShapeDtypeStruct(h.shape, F32),
        scratch_shapes=[pltpu.VMEM((SUBLANES, CONF_CARRY_ROWS + rows, D_MODEL), F32),
                        pltpu.VMEM((rows, D_MODEL), F32)],
        compiler_params=pltpu.CompilerParams(
            dimension_semantics=("arbitrary", "arbitrary"), vmem_limit_bytes=VMEM_LIMIT),
        name="odd_mixer",
    )(h, *consts)


def _pack_halves(v):
    n = v.shape[1] // 2
    lo = lax.bitcast_convert_type(v[:, :n].astype(BF16).astype(F32), jnp.uint32)
    hi = lax.bitcast_convert_type(v[:, n:].astype(BF16).astype(F32), jnp.uint32)
    return (lo >> 16) | (hi & jnp.uint32(0xFFFF0000))


def _unpack_halves(w):
    lo = lax.bitcast_convert_type(w << 16, F32)
    hi = lax.bitcast_convert_type(w & jnp.uint32(0xFFFF0000), F32)
    return lo, hi


def _router_kernel(x_ref, w_hi_ref, w_lo_ref, b_ref, idx_ref, gate_ref, rank_ref, cnt_ref, xp_ref,
                   carry_ref):
    rows = x_ref.shape[0]
    i = pl.program_id(0)

    @pl.when(i == 0)
    def _():
        carry_ref[...] = jnp.zeros(carry_ref.shape, F32)

    x = x_ref[...]
    x1 = x.astype(BF16)
    x2 = (x - x1.astype(F32)).astype(BF16)
    logits = (_dot(x1, w_hi_ref[...]) + (_dot(x1, w_lo_ref[...]) + _dot(x2, w_hi_ref[...]))
              + b_ref[...])
    xp_ref[...] = _pack_halves(x)
    lane = lax.broadcasted_iota(jnp.int32, (rows, LANES), 1)
    lane_f = lane.astype(F32)
    work = logits
    vals, sels, idxs = [], [], []
    for _ in range(TOP_K):
        m = jnp.max(work, axis=-1, keepdims=True)
        first = jnp.min(jnp.where(work == m, lane_f, float(LANES)), axis=-1, keepdims=True)
        sel = lane_f == first
        work = jnp.where(sel, -jnp.inf, work)
        vals.append(m)
        sels.append(sel)
        idxs.append(first)
    exps = [jnp.exp(v - vals[0]) for v in vals]
    inv = 1.0 / (exps[0] + exps[1] + exps[2] + exps[3])
    onehot = jnp.zeros((rows, LANES), F32)
    for sel in sels:
        onehot = onehot + jnp.where(sel, 1.0, 0.0)
    ri = lax.broadcasted_iota(jnp.int32, (rows, rows), 0)
    ci = lax.broadcasted_iota(jnp.int32, (rows, rows), 1)
    strict = jnp.where(ri > ci, 1.0, 0.0).astype(BF16)
    before = _dot(strict, onehot.astype(BF16)) + carry_ref[0:1, :]
    idx_o = jnp.zeros((rows, LANES), F32)
    gate_o = jnp.zeros((rows, LANES), F32)
    rank_o = jnp.zeros((rows, LANES), F32)
    for k in range(TOP_K):
        rank_k = jnp.sum(jnp.where(sels[k], before, 0.0), axis=-1, keepdims=True)
        here = lane == k
        idx_o = jnp.where(here, idxs[k], idx_o)
        gate_o = jnp.where(here, exps[k] * inv, gate_o)
        rank_o = jnp.where(here, rank_k, rank_o)
    gate_ref[...] = gate_o
    idx_ref[...] = idx_o.T[0:SUBLANES, :].astype(jnp.int32)
    rank_ref[...] = rank_o.T[0:SUBLANES, :].astype(jnp.int32)
    total = carry_ref[0:1, :] + jnp.sum(onehot, axis=0, keepdims=True)
    carry_ref[...] = jnp.broadcast_to(total, carry_ref.shape)
    cnt_ref[...] = jnp.broadcast_to(total, cnt_ref.shape).astype(jnp.int32)


def _router(xt, w_r, b_r):
    t = xt.shape[0]
    rows = min(ROUTER_ROWS, t)
    pad = LANES - N_EXPERTS
    w = jnp.pad(w_r, ((0, 0), (0, pad)))
    w_hi = w.astype(BF16)
    w_lo = (w - w_hi.astype(F32)).astype(BF16)
    b = jnp.pad(b_r, (0, pad), constant_values=-1e30).reshape(1, LANES)
    km_spec = pl.BlockSpec((SUBLANES, rows), lambda i: (0, i))
    idx, gate, rank, cnt, xp = pl.pallas_call(
        _router_kernel,
        grid=(t // rows,),
        in_specs=[pl.BlockSpec((rows, D_MODEL), lambda i: (i, 0)), _const_spec(w.shape),
                  _const_spec(w.shape), _const_spec(b.shape)],
        out_specs=[km_spec, pl.BlockSpec((rows, LANES), lambda i: (i, 0)), km_spec,
                   pl.BlockSpec((SUBLANES, LANES), lambda i: (0, 0)),
                   pl.BlockSpec((rows, D_MODEL // 2), lambda i: (i, 0))],
        out_shape=[jax.ShapeDtypeStruct((SUBLANES, t), jnp.int32),
                   jax.ShapeDtypeStruct((t, LANES), F32),
                   jax.ShapeDtypeStruct((SUBLANES, t), jnp.int32),
                   jax.ShapeDtypeStruct((SUBLANES, LANES), jnp.int32),
                   jax.ShapeDtypeStruct((t, D_MODEL // 2), jnp.uint32)],
        scratch_shapes=[pltpu.VMEM((SUBLANES, LANES), F32)],
        compiler_params=pltpu.CompilerParams(dimension_semantics=("arbitrary",),
                                             vmem_limit_bytes=VMEM_LIMIT),
        name="moe_router",
    )(xt, w_hi, w_lo, b)
    return idx[:TOP_K], gate, rank[:TOP_K], cnt[0, :N_EXPERTS], xp


def _sc_workers():
    mesh = plsc.VectorSubcoreMesh(core_axis_name="c", subcore_axis_name="s")
    return mesh, mesh.num_cores, mesh.num_cores * mesh.num_subcores


def _dispatch(xt, dest_km, n_rows):
    t = xt.shape[0]
    mesh, n_cores, n_workers = _sc_workers()
    n_ch = t // n_workers // SC_ROWS
    assert n_ch * SC_ROWS * n_workers == t and n_ch % 2 == 0

    def body(x_hbm, dest_hbm, xs_hbm, idx_v, rows_v, load_sem, scat_sem):
        wid = lax.axis_index("s") * n_cores + lax.axis_index("c")
        base = wid * (n_ch * SC_ROWS)
        for k in range(TOP_K):
            pltpu.sync_copy(dest_hbm.at[pl.ds(k * (t // SC_ROWS) + wid * n_ch, n_ch)], idx_v.at[k])

        def load(j, b):
            off = pl.multiple_of(base + j * SC_ROWS, SC_ROWS)
            return pltpu.make_async_copy(x_hbm.at[pl.ds(off, SC_ROWS)], rows_v.at[b], load_sem.at[b])

        def scatter(j, b, k):
            return pltpu.make_async_copy(rows_v.at[b], xs_hbm.at[idx_v.at[k, j]], scat_sem.at[b])

        load(0, 0).start()

        @pl.loop(0, n_ch, step=2)
        def _(j0):
            for b in range(2):
                j = j0 + b
                load(j, b).wait()

                @pl.when(j >= 1)
                def _():
                    for k in range(TOP_K):
                        scatter(j - 1, 1 - b, k).wait()

                @pl.when(j + 1 < n_ch)
                def _():
                    load(j + 1, 1 - b).start()

                for k in range(TOP_K):
                    scatter(j, b, k).start()

        for k in range(TOP_K):
            scatter(n_ch - 1, 1, k).wait()

    return pl.kernel(
        body, mesh=mesh,
        out_type=jax.ShapeDtypeStruct((n_rows, xt.shape[1]), xt.dtype),
        scratch_types=[pltpu.VMEM((TOP_K, n_ch, SC_ROWS), jnp.int32),
                       pltpu.VMEM((2, SC_ROWS, xt.shape[1]), xt.dtype),
                       pltpu.SemaphoreType.DMA((2,)), pltpu.SemaphoreType.DMA((2,))],
    )(xt, dest_km)


def _gather_rows(yb, dest_km):
    n_idx = dest_km.shape[0] * SC_ROWS
    mesh, n_cores, n_workers = _sc_workers()
    n_ch = n_idx // n_workers // SC_ROWS
    assert n_ch * SC_ROWS * n_workers == n_idx and n_ch % 2 == 0

    def body(yb_hbm, dest_hbm, out_hbm, idx_v, rows_v, gat_sem, store_sem):
        wid = lax.axis_index("s") * n_cores + lax.axis_index("c")
        base = wid * (n_ch * SC_ROWS)
        pltpu.sync_copy(dest_hbm.at[pl.ds(wid * n_ch, n_ch)], idx_v)

        def gather(j, b):
            return pltpu.make_async_copy(yb_hbm.at[idx_v.at[j]], rows_v.at[b], gat_sem.at[b])

        def store(j, b):
            off = pl.multiple_of(base + j * SC_ROWS, SC_ROWS)
            return pltpu.make_async_copy(rows_v.at[b], out_hbm.at[pl.ds(off, SC_ROWS)], store_sem.at[b])

        gather(0, 0).start()

        @pl.loop(0, n_ch, step=2)
        def _(j0):
            for b in range(2):
                j = j0 + b
                gather(j, b).wait()

                @pl.when(j >= 1)
                def _():
                    store(j - 1, 1 - b).wait()

                @pl.when(j + 1 < n_ch)
                def _():
                    gather(j + 1, 1 - b).start()

                store(j, b).start()

        store(n_ch - 1, 1).wait()

    return pl.kernel(
        body, mesh=mesh,
        out_type=jax.ShapeDtypeStruct((n_idx, yb.shape[1]), yb.dtype),
        scratch_types=[pltpu.VMEM((n_ch, SC_ROWS), jnp.int32),
                       pltpu.VMEM((2, SC_ROWS, yb.shape[1]), yb.dtype),
                       pltpu.SemaphoreType.DMA((2,)), pltpu.SemaphoreType.DMA((2,))],
    )(yb, dest_km)


def _expert_kernel(be_ref, nused_ref, valid_ref, fresh_ref, x_ref, wgu_ref, bgu_ref, wd_ref, bd_ref,
                   out_ref, wgu_bf, wd_bf):
    del be_ref
    i = pl.program_id(0)
    used = i < nused_ref[0]

    @pl.when(jnp.logical_and(used, fresh_ref[i] == 1))
    def _():
        wgu_bf[...] = wgu_ref[0, 0].astype(BF16)
        wd_bf[...] = wd_ref[0, 0].astype(BF16)

    @pl.when(used)
    def _():
        row = lax.broadcasted_iota(jnp.int32, x_ref.shape, 0)
        x_lo, x_hi = _unpack_halves(jnp.where(row < valid_ref[i], x_ref[...], jnp.uint32(0)))
        half = x_ref.shape[1]
        gu = (_dot(x_lo.astype(BF16), wgu_bf[0:half, :])
              + _dot(x_hi.astype(BF16), wgu_bf[half:, :]) + bgu_ref[0, 0])
        d_exp = wd_ref.shape[2]
        gate = jnp.minimum(gu[:, :d_exp], SWIGLU_LIMIT)
        up = jnp.clip(gu[:, d_exp:], -SWIGLU_LIMIT, SWIGLU_LIMIT)
        act = (up + 1.0) * (gate * jax.nn.sigmoid(SWIGLU_ALPHA * gate))
        out_ref[...] = _pack_halves(_dot(act.astype(BF16), wd_bf[...]) + bd_ref[0, 0])

    @pl.when(i >= nused_ref[0])
    def _():
        out_ref[...] = jnp.zeros(out_ref.shape, jnp.uint32)


def _experts(xs, block_e, n_used, valid, fresh, layer, w_gu, b_gu, w_down, b_down):
    n_rows = xs.shape[0]
    n_blocks = n_rows // MOE_BLOCK
    depth, n_exp, d_exp = w_down.shape[:3]

    def of_expert(i, be, *_):
        return (layer, be[i], 0, 0)

    grid_spec = pltpu.PrefetchScalarGridSpec(
        num_scalar_prefetch=4,
        grid=(n_blocks,),
        in_specs=[
            pl.BlockSpec((MOE_BLOCK, D_MODEL // 2), lambda i, be, nu, *_: (jnp.minimum(i, nu[0] - 1), 0)),
            pl.BlockSpec((1, 1, D_MODEL, 2 * d_exp), of_expert),
            pl.BlockSpec((1, 1, 1, 2 * d_exp), of_expert),
            pl.BlockSpec((1, 1, d_exp, D_MODEL), of_expert),
            pl.BlockSpec((1, 1, 1, D_MODEL), of_expert),
        ],
        out_specs=pl.BlockSpec((MOE_BLOCK, D_MODEL // 2), lambda i, *_: (i, 0)),
        scratch_shapes=[pltpu.VMEM((D_MODEL, 2 * d_exp), BF16), pltpu.VMEM((d_exp, D_MODEL), BF16)],
    )
    return pl.pallas_call(
        _expert_kernel,
        grid_spec=grid_spec,
        out_shape=jax.ShapeDtypeStruct((n_rows, D_MODEL // 2), jnp.uint32),
        compiler_params=pltpu.CompilerParams(dimension_semantics=("arbitrary",),
                                             vmem_limit_bytes=VMEM_LIMIT),
        name="moe_experts",
    )(block_e, n_used, valid, fresh, xs, w_gu, b_gu.reshape(depth, n_exp, 1, -1), w_down,
      b_down.reshape(depth, n_exp, 1, -1))


def _combine_kernel(h_ref, gate_ref, y4_ref, lng_ref, lnb_ref, out_ref):
    gates = gate_ref[...]
    y_lo, y_hi = None, None
    for k in range(TOP_K):
        lo, hi = _unpack_halves(y4_ref[k])
        g = gates[:, k:k + 1]
        y_lo = g * lo if y_lo is None else y_lo + g * lo
        y_hi = g * hi if y_hi is None else y_hi + g * hi
    y = jnp.concatenate([y_lo, y_hi], axis=1)
    out_ref[...] = _layer_norm(ALPHA * h_ref[...] + y, lng_ref[...], lnb_ref[...])


def _combine(xt, gates, y4, ln_g, ln_b):
    t = xt.shape[0]
    rows = min(COMBINE_ROWS, t)
    tok_spec = pl.BlockSpec((rows, D_MODEL), lambda i: (i, 0))
    return pl.pallas_call(
        _combine_kernel,
        grid=(t // rows,),
        in_specs=[tok_spec,
                  pl.BlockSpec((rows, LANES), lambda i: (i, 0)),
                  pl.BlockSpec((TOP_K, rows, D_MODEL // 2), lambda i: (0, i, 0)),
                  _const_spec((1, D_MODEL)), _const_spec((1, D_MODEL))],
        out_specs=tok_spec,
        out_shape=jax.ShapeDtypeStruct((t, D_MODEL), F32),
        compiler_params=pltpu.CompilerParams(dimension_semantics=("arbitrary",),
                                             vmem_limit_bytes=VMEM_LIMIT),
        name="moe_combine",
    )(xt, gates, y4, ln_g.reshape(1, -1), ln_b.reshape(1, -1))


def _moe_layer(h, layer, w_r, b_r, w_gu, b_gu, w_down, b_down, ln_g, ln_b):
    bsz, seq, d = h.shape
    t = bsz * seq
    xt = h.reshape(t, d)
    idx, gates, rank, counts, xp = _router(xt, w_r, b_r)
    experts = jnp.arange(N_EXPERTS, dtype=jnp.int32)
    padded = ((counts + MOE_BLOCK - 1) // MOE_BLOCK) * MOE_BLOCK
    pad_ends = jnp.sum(jnp.where(experts[None, :] <= experts[:, None], padded[None, :], 0), axis=1)
    pad_starts = pad_ends - padded
    start_of = jnp.zeros_like(idx)
    for e in range(N_EXPERTS):
        start_of = jnp.where(idx == e, pad_starts[e], start_of)
    dest_km = (start_of + rank).reshape(t * TOP_K // SC_ROWS, SC_ROWS)
    n_blocks = -(-(t * TOP_K) // MOE_BLOCK) + N_EXPERTS
    block_lo = jnp.arange(n_blocks, dtype=jnp.int32) * MOE_BLOCK
    block_e = jnp.minimum(jnp.sum(pad_ends[None, :] <= block_lo[:, None], axis=1),
                          N_EXPERTS - 1).astype(jnp.int32)
    n_used = jnp.sum(padded, keepdims=True).astype(jnp.int32) // MOE_BLOCK
    of_block = block_e[:, None] == experts[None, :]
    rows_left = jnp.sum(jnp.where(of_block, (counts + pad_starts)[None, :], 0), axis=1) - block_lo
    valid = jnp.clip(rows_left, 0, MOE_BLOCK).astype(jnp.int32)
    fresh = jnp.concatenate([jnp.ones((1,), jnp.int32),
                             (block_e[1:] != block_e[:-1]).astype(jnp.int32)])
    xs = _dispatch(xp, dest_km, n_blocks * MOE_BLOCK)
    yb = _experts(xs, block_e, n_used, valid, fresh, layer, w_gu, b_gu, w_down, b_down)
    y4 = _gather_rows(yb, dest_km).reshape(TOP_K, t, d // 2)
    out = _combine(xt, gates, y4, ln_g, ln_b)
    return out.reshape(bsz, seq, d)


def kernel(x, mix_w_in, sc_conv_w, ssm_conv_w, ssm_conv_b, ssm_dt_bias, ssm_a_log, ssm_d, ssm_norm_w, mix_w_out, conf_w_pw1, conf_b_pw1, conf_w_dw, conf_b_dw, conf_ln_g, conf_ln_b, conf_w_pw2, conf_b_pw2, router_w, router_b, exp_w_gu, exp_b_gu, exp_w_down, exp_b_down, ln_mix_g, ln_mix_b, ln_ffn_g, ln_ffn_b):
    h = x
    for i in range(DEPTH):
        j = i // 2
        if i % 2 == 0:
            h = _even_mixer(h, mix_w_in[j], sc_conv_w[j], ssm_conv_w[j], ssm_conv_b[j],
                            ssm_dt_bias[j], ssm_a_log[j], ssm_d[j], ssm_norm_w[j], mix_w_out[j],
                            ln_mix_g[i], ln_mix_b[i])
        else:
            h = _odd_mixer(h, conf_w_pw1[j], conf_b_pw1[j], conf_w_dw[j], conf_b_dw[j],
                           conf_ln_g[j], conf_ln_b[j], conf_w_pw2[j], conf_b_pw2[j],
                           ln_mix_g[i], ln_mix_b[i])
        h = _moe_layer(h, i, router_w[i], router_b[i], exp_w_gu, exp_b_gu, exp_w_down,
                       exp_b_down, ln_ffn_g[i], ln_ffn_b[i])
    return h
```

```python
import functools
import math

import jax
import jax.numpy as jnp
from jax import lax
from jax.experimental import pallas as pl
from jax.experimental.pallas import tpu as pltpu
from jax.experimental.pallas import tpu_sc as plsc

F32 = jnp.float32
BF16 = jnp.bfloat16

D_MODEL = 1024
DEPTH = 4
ALPHA = (2 * DEPTH) ** 0.25
LN_EPS = 1e-5
SC_KERNEL = 3
SSM_HEAD_DIM = 64
SSM_HEADS = 16
SSM_GROUPS = 2
SSM_STATE = 128
SSM_CONV = 4
SSM_CHUNK = 128
SSM_INNER = D_MODEL
GROUP_COLS = SSM_INNER // SSM_GROUPS
XBC_COLS = SSM_INNER + 2 * SSM_GROUPS * SSM_STATE
CONF_KERNEL = 31
N_EXPERTS = 32
TOP_K = 4
SWIGLU_LIMIT = 7.0
SWIGLU_ALPHA = 1.702
MOE_BLOCK = 512

LANES = 128
SUBLANES = 8
CARRY_ROWS = 8
CONF_CARRY_ROWS = 32
CONF_TILE_ROWS = 32
VMEM_LIMIT = 56 * 1024 * 1024

MIX_ROWS = 256
SSD_MIX_ROWS = 512
ROUTER_ROWS = 512
COMBINE_ROWS = 256
SC_ROWS = 64


def _layer_norm(v, g, b):
    mu = jnp.mean(v, axis=-1, keepdims=True)
    d = v - mu
    var = jnp.mean(d * d, axis=-1, keepdims=True)
    return d * lax.rsqrt(var + LN_EPS) * g + b


def _silu(v):
    return v * jax.nn.sigmoid(v)


def _softplus(v):
    return jnp.maximum(v, 0.0) + jnp.log1p(jnp.exp(-jnp.abs(v)))


def _dot(a, b):
    return jnp.dot(a, b, preferred_element_type=F32)


def _dot_exact01(m01, v):
    v1 = v.astype(BF16)
    r1 = v - v1.astype(F32)
    v2 = r1.astype(BF16)
    v3 = (r1 - v2.astype(F32)).astype(BF16)
    return _dot(m01, v1) + _dot(m01, v2) + _dot(m01, v3)


def _even_mixer_kernel(x_ref, w_sc_ref, w_z_ref, w_xbc_ref, w_dt_ref, scw_ref, ssmw_ref,
                       ssmb_ref, dtb_ref, a_ref, dexp_ref, normw_ref, wout_ref,
                       lng_ref, lnb_ref, out_ref,
                       u_buf, xbc_buf, act_buf, dtn_buf, y_buf, state_ref):
    rows = x_ref.shape[1]
    c = pl.program_id(1)

    @pl.when(c == 0)
    def _():
        u_buf[0:CARRY_ROWS, :] = jnp.zeros((CARRY_ROWS, D_MODEL), F32)
        xbc_buf[0:CARRY_ROWS, :] = jnp.zeros((CARRY_ROWS, XBC_COLS), F32)
        state_ref[...] = jnp.zeros(state_ref.shape, F32)

    x = x_ref[0]
    xb = x.astype(BF16)

    p = _dot(xb, w_sc_ref[...])
    u = p[:, D_MODEL:2 * D_MODEL] * p[:, :D_MODEL]
    u_buf[CARRY_ROWS:CARRY_ROWS + rows, :] = u
    conv = scw_ref[2:3, :] * u
    for k in range(SC_KERNEL - 1):
        shift = SC_KERNEL - 1 - k
        conv = conv + scw_ref[k:k + 1, :] * u_buf[CARRY_ROWS - shift:CARRY_ROWS - shift + rows, :]
    y_sc = p[:, 2 * D_MODEL:] * conv
    u_buf[0:CARRY_ROWS, :] = u[rows - CARRY_ROWS:, :]

    xbc = _dot(xb, w_xbc_ref[...])
    xbc_buf[CARRY_ROWS:CARRY_ROWS + rows, :] = xbc
    cv = ssmw_ref[SSM_CONV - 1:SSM_CONV, :] * xbc + ssmb_ref[...]
    for k in range(SSM_CONV - 1):
        shift = SSM_CONV - 1 - k
        cv = cv + ssmw_ref[k:k + 1, :] * xbc_buf[CARRY_ROWS - shift:CARRY_ROWS - shift + rows, :]
    act_buf[...] = _silu(cv)
    xbc_buf[0:CARRY_ROWS, :] = xbc[rows - CARRY_ROWS:, :]

    dtn_buf[...] = _softplus(_dot(xb, w_dt_ref[...]) + dtb_ref[...])

    li = lax.broadcasted_iota(jnp.int32, (SSM_CHUNK, SSM_CHUNK), 0)
    si = lax.broadcasted_iota(jnp.int32, (SSM_CHUNK, SSM_CHUNK), 1)
    causal = li >= si
    tril = jnp.where(causal, 1.0, 0.0).astype(BF16)
    low_half = si < SSM_HEAD_DIM
    low_row = low_half[0:1, :]

    for j in range(rows // SSM_CHUNK):
        r0 = j * SSM_CHUNK
        dt_c = dtn_buf[r0:r0 + SSM_CHUNK, :]
        acum = _dot_exact01(tril, dt_c * a_ref[...])
        alast = acum[SSM_CHUNK - 1:SSM_CHUNK, :]
        acum_t = acum.T
        dt_t = dt_c.T
        e_acum = jnp.exp(acum)
        w_t = (jnp.exp(alast - acum) * dt_c).T
        chunk_decay = jnp.exp(alast)
        xs_b = act_buf[r0:r0 + SSM_CHUNK, 0:SSM_INNER].astype(BF16)
        for g in range(SSM_GROUPS):
            b_g = act_buf[r0:r0 + SSM_CHUNK, SSM_INNER + g * SSM_STATE:SSM_INNER + (g + 1) * SSM_STATE]
            c_lo = SSM_INNER + (SSM_GROUPS + g) * SSM_STATE
            c_g = act_buf[r0:r0 + SSM_CHUNK, c_lo:c_lo + SSM_STATE]
            cb = lax.dot_general(c_g.astype(BF16), b_g.astype(BF16), (((1,), (1,)), ((), ())),
                                 preferred_element_type=F32)
            b_gt = b_g.T
            for pair in range(GROUP_COLS // LANES):
                col0 = g * GROUP_COLS + pair * LANES
                h0 = col0 // SSM_HEAD_DIM
                xs_p = xs_b[:, col0:col0 + LANES]
                zero_b = jnp.zeros_like(xs_p)
                xs_lo = jnp.where(low_half, xs_p, zero_b)
                xs_hi = jnp.where(low_half, zero_b, xs_p)
                st = state_ref[g, :, pair * LANES:(pair + 1) * LANES]
                st_b = st.astype(BF16)
                lhs, lhs_s = [], []
                for h in (h0, h0 + 1):
                    seg = acum[:, h:h + 1] - acum_t[h:h + 1, :]
                    decay = jnp.exp(jnp.where(causal, seg, -jnp.inf))
                    lhs.append((cb * decay * dt_t[h:h + 1, :]).astype(BF16))
                    lhs_s.append((b_gt * w_t[h:h + 1, :]).astype(BF16))
                for h in (h0, h0 + 1):
                    lhs.append((c_g * e_acum[:, h:h + 1]).astype(BF16))
                rhs = jnp.concatenate([xs_lo, xs_hi, jnp.where(low_half, st_b, zero_b),
                                       jnp.where(low_half, zero_b, st_b)], axis=0)
                y_buf[r0:r0 + SSM_CHUNK, col0:col0 + LANES] = _dot(
                    jnp.concatenate(lhs, axis=1), rhs)
                cd_p = jnp.where(low_row, chunk_decay[:, h0:h0 + 1], chunk_decay[:, h0 + 1:h0 + 2])
                state_ref[g, :, pair * LANES:(pair + 1) * LANES] = st * cd_p + _dot(
                    jnp.concatenate(lhs_s, axis=1), jnp.concatenate([xs_lo, xs_hi], axis=0))

    z = _dot(xb, w_z_ref[...])
    y = (y_buf[...] + act_buf[:, 0:SSM_INNER] * dexp_ref[...]) * _silu(z)
    parts = []
    for g in range(SSM_GROUPS):
        yg = y[:, g * GROUP_COLS:(g + 1) * GROUP_COLS]
        ms = jnp.mean(yg * yg, axis=-1, keepdims=True)
        parts.append(yg * lax.rsqrt(ms + LN_EPS) * normw_ref[:, g * GROUP_COLS:(g + 1) * GROUP_COLS])
    mix = _dot(y_sc.astype(BF16), wout_ref[0:D_MODEL, :])
    for g in range(SSM_GROUPS):
        lo = D_MODEL + g * GROUP_COLS
        mix = mix + _dot(parts[g].astype(BF16), wout_ref[lo:lo + GROUP_COLS, :])
    out_ref[0] = _layer_norm(ALPHA * x + mix, lng_ref[...], lnb_ref[...])


def _const_spec(shape):
    nd = len(shape)
    return pl.BlockSpec(shape, lambda *_: (0,) * nd, pipeline_mode=pl.Buffered(1))


def _even_mixer(h, w_in, sc_conv_w, ssm_conv_w, ssm_conv_b, dt_bias, a_log, d_skip, norm_w,
                w_out, ln_g, ln_b):
    bsz, seq, _ = h.shape
    rows = min(SSD_MIX_ROWS, seq)
    cut_z = 3 * D_MODEL
    cut_xbc = cut_z + SSM_INNER
    cut_dt = cut_xbc + XBC_COLS
    w_sc = w_in[:, :cut_z].astype(BF16)
    w_z = w_in[:, cut_z:cut_xbc].astype(BF16)
    w_xbc = w_in[:, cut_xbc:cut_dt].astype(BF16)
    pad = LANES - SSM_HEADS
    w_dt = jnp.pad(w_in[:, cut_dt:], ((0, 0), (0, pad))).astype(BF16)
    dtb = jnp.pad(dt_bias, (0, pad)).reshape(1, LANES)
    a = -jnp.exp(a_log)
    a_n = jnp.pad(a, (0, pad)).reshape(1, LANES)
    d_exp = jnp.repeat(d_skip, SSM_HEAD_DIM).reshape(1, SSM_INNER)
    consts = [w_sc, w_z, w_xbc, w_dt, sc_conv_w, ssm_conv_w, ssm_conv_b.reshape(1, -1), dtb, a_n,
              d_exp, norm_w.reshape(1, -1), w_out.astype(BF16), ln_g.reshape(1, -1),
              ln_b.reshape(1, -1)]
    x_spec = pl.BlockSpec((1, rows, D_MODEL), lambda b, c: (b, c, 0))
    return pl.pallas_call(
        _even_mixer_kernel,
        grid=(bsz, seq // rows),
        in_specs=[x_spec] + [_const_spec(v.shape) for v in consts],
        out_specs=x_spec,
        out_shape=jax.ShapeDtypeStruct(h.shape, F32),
        scratch_shapes=[
            pltpu.VMEM((CARRY_ROWS + rows, D_MODEL), F32),
            pltpu.VMEM((CARRY_ROWS + rows, XBC_COLS), F32),
            pltpu.VMEM((rows, XBC_COLS), F32),
            pltpu.VMEM((rows, LANES), F32),
            pltpu.VMEM((rows, SSM_INNER), F32),
            pltpu.VMEM((SSM_GROUPS, SSM_STATE, GROUP_COLS), F32),
        ],
        compiler_params=pltpu.CompilerParams(
            dimension_semantics=("arbitrary", "arbitrary"), vmem_limit_bytes=VMEM_LIMIT),
        name="even_mixer",
    )(h, *consts)


def _odd_mixer_kernel(x_ref, w1_ref, b1_ref, wdw_ref, bdw_ref, g_ref, b_ref, w2_ref, b2_ref,
                      lng_ref, lnb_ref, out_ref, sh_buf, conv_buf):
    rows = x_ref.shape[1]
    c = pl.program_id(1)

    @pl.when(c == 0)
    def _():
        sh_buf[0, 0:CONF_CARRY_ROWS, :] = jnp.zeros((CONF_CARRY_ROWS, D_MODEL), F32)

    x = x_ref[0]
    u = _dot(x.astype(BF16), w1_ref[...]) + b1_ref[...]
    u = u[:, :D_MODEL] * jax.nn.sigmoid(u[:, D_MODEL:])
    sh_buf[0, CONF_CARRY_ROWS:CONF_CARRY_ROWS + rows, :] = u
    span = rows + CONF_CARRY_ROWS - SUBLANES
    for r in range(1, SUBLANES):
        sh_buf[r, 0:span, :] = sh_buf[0, r:r + span, :]

    def conv_tile(i, carry):
        base = pl.multiple_of(i * CONF_TILE_ROWS, CONF_TILE_ROWS)
        groups = CONF_TILE_ROWS // SUBLANES
        acc = [bdw_ref[...] for _ in range(groups)]
        for k in range(CONF_KERNEL):
            lo = CONF_CARRY_ROWS - (CONF_KERNEL - 1 - k)
            w_k = wdw_ref[k]
            for g in range(groups):
                row0 = base + (lo // SUBLANES + g) * SUBLANES
                acc[g] = acc[g] + w_k * sh_buf[lo % SUBLANES, pl.ds(row0, SUBLANES), :]
        for g in range(groups):
            conv_buf[pl.ds(base + g * SUBLANES, SUBLANES), :] = acc[g]
        return carry

    lax.fori_loop(0, rows // CONF_TILE_ROWS, conv_tile, 0)
    sh_buf[0, 0:CONF_CARRY_ROWS, :] = u[rows - CONF_CARRY_ROWS:, :]
    v = _silu(_layer_norm(conv_buf[...], g_ref[...], b_ref[...]))
    mix = _dot(v.astype(BF16), w2_ref[...]) + b2_ref[...]
    out_ref[0] = _layer_norm(ALPHA * x + mix, lng_ref[...], lnb_ref[...])


def _odd_mixer(h, w_pw1, b_pw1, w_dw, b_dw, ln_g_c, ln_b_c, w_pw2, b_pw2, ln_g, ln_b):
    bsz, seq, _ = h.shape
    rows = min(MIX_ROWS, seq)
    w_dw_rows = jnp.broadcast_to(w_dw[:, None, :], (CONF_KERNEL, SUBLANES, D_MODEL))
    b_dw_rows = jnp.broadcast_to(b_dw[None, :], (SUBLANES, D_MODEL))
    consts = [w_pw1.astype(BF16), b_pw1.reshape(1, -1), w_dw_rows, b_dw_rows,
              ln_g_c.reshape(1, -1), ln_b_c.reshape(1, -1), w_pw2.astype(BF16),
              b_pw2.reshape(1, -1), ln_g.reshape(1, -1), ln_b.reshape(1, -1)]
    x_spec = pl.BlockSpec((1, rows, D_MODEL), lambda b, c: (b, c, 0))
    return pl.pallas_call(
        _odd_mixer_kernel,
        grid=(bsz, seq // rows),
        in_specs=[x_spec] + [_const_spec(v.shape) for v in consts],
        out_specs=x_spec,
        out_shape=jax.ShapeDtypeStruct(h.shape, F32),
        scratch_shapes=[pltpu.VMEM((SUBLANES, CONF_CARRY_ROWS + rows, D_MODEL), F32),
                        pltpu.VMEM((rows, D_MODEL), F32)],
        compiler_params=pltpu.CompilerParams(
            dimension_semantics=("arbitrary", "arbitrary"), vmem_limit_bytes=VMEM_LIMIT),
        name="odd_mixer",
    )(h, *consts)


def _pack_halves(v):
    n = v.shape[1] // 2
    lo = lax.bitcast_convert_type(v[:, :n].astype(BF16).astype(F32), jnp.uint32)
    hi = lax.bitcast_convert_type(v[:, n:].astype(BF16).astype(F32), jnp.uint32)
    return (lo >> 16) | (hi & jnp.uint32(0xFFFF0000))


def _unpack_halves(w):
    lo = lax.bitcast_convert_type(w << 16, F32)
    hi = lax.bitcast_convert_type(w & jnp.uint32(0xFFFF0000), F32)
    return lo, hi


def _router_kernel(x_ref, w_hi_ref, w_lo_ref, b_ref, idx_ref, gate_ref, rank_ref, cnt_ref, xp_ref,
                   carry_ref):
    rows = x_ref.shape[0]
    i = pl.program_id(0)

    @pl.when(i == 0)
    def _():
        carry_ref[...] = jnp.zeros(carry_ref.shape, F32)

    x = x_ref[...]
    x1 = x.astype(BF16)
    x2 = (x - x1.astype(F32)).astype(BF16)
    logits = (_dot(x1, w_hi_ref[...]) + (_dot(x1, w_lo_ref[...]) + _dot(x2, w_hi_ref[...]))
              + b_ref[...])
    xp_ref[...] = _pack_halves(x)
    lane = lax.broadcasted_iota(jnp.int32, (rows, LANES), 1)
    lane_f = lane.astype(F32)
    work = logits
    vals, sels, idxs = [], [], []
    for _ in range(TOP_K):
        m = jnp.max(work, axis=-1, keepdims=True)
        first = jnp.min(jnp.where(work == m, lane_f, float(LANES)), axis=-1, keepdims=True)
        sel = lane_f == first
        work = jnp.where(sel, -jnp.inf, work)
        vals.append(m)
        sels.append(sel)
        idxs.append(first)
    exps = [jnp.exp(v - vals[0]) for v in vals]
    inv = 1.0 / (exps[0] + exps[1] + exps[2] + exps[3])
    onehot = jnp.zeros((rows, LANES), F32)
    for sel in sels:
        onehot = onehot + jnp.where(sel, 1.0, 0.0)
    ri = lax.broadcasted_iota(jnp.int32, (rows, rows), 0)
    ci = lax.broadcasted_iota(jnp.int32, (rows, rows), 1)
    strict = jnp.where(ri > ci, 1.0, 0.0).astype(BF16)
    before = _dot(strict, onehot.astype(BF16)) + carry_ref[0:1, :]
    idx_o = jnp.zeros((rows, LANES), F32)
    gate_o = jnp.zeros((rows, LANES), F32)
    rank_o = jnp.zeros((rows, LANES), F32)
    for k in range(TOP_K):
        rank_k = jnp.sum(jnp.where(sels[k], before, 0.0), axis=-1, keepdims=True)
        here = lane == k
        idx_o = jnp.where(here, idxs[k], idx_o)
        gate_o = jnp.where(here, exps[k] * inv, gate_o)
        rank_o = jnp.where(here, rank_k, rank_o)
    gate_ref[...] = gate_o
    idx_ref[...] = idx_o.T[0:SUBLANES, :].astype(jnp.int32)
    rank_ref[...] = rank_o.T[0:SUBLANES, :].astype(jnp.int32)
    total = carry_ref[0:1, :] + jnp.sum(onehot, axis=0, keepdims=True)
    carry_ref[...] = jnp.broadcast_to(total, carry_ref.shape)
    cnt_ref[...] = jnp.broadcast_to(total, cnt_ref.shape).astype(jnp.int32)


def _router(xt, w_r, b_r):
    t = xt.shape[0]
    rows = min(ROUTER_ROWS, t)
    pad = LANES - N_EXPERTS
    w = jnp.pad(w_r, ((0, 0), (0, pad)))
    w_hi = w.astype(BF16)
    w_lo = (w - w_hi.astype(F32)).astype(BF16)
    b = jnp.pad(b_r, (0, pad), constant_values=-1e30).reshape(1, LANES)
    km_spec = pl.BlockSpec((SUBLANES, rows), lambda i: (0, i))
    idx, gate, rank, cnt, xp = pl.pallas_call(
        _router_kernel,
        grid=(t // rows,),
        in_specs=[pl.BlockSpec((rows, D_MODEL), lambda i: (i, 0)), _const_spec(w.shape),
                  _const_spec(w.shape), _const_spec(b.shape)],
        out_specs=[km_spec, pl.BlockSpec((rows, LANES), lambda i: (i, 0)), km_spec,
                   pl.BlockSpec((SUBLANES, LANES), lambda i: (0, 0)),
                   pl.BlockSpec((rows, D_MODEL // 2), lambda i: (i, 0))],
        out_shape=[jax.ShapeDtypeStruct((SUBLANES, t), jnp.int32),
                   jax.ShapeDtypeStruct((t, LANES), F32),
                   jax.ShapeDtypeStruct((SUBLANES, t), jnp.int32),
                   jax.ShapeDtypeStruct((SUBLANES, LANES), jnp.int32),
                   jax.ShapeDtypeStruct((t, D_MODEL // 2), jnp.uint32)],
        scratch_shapes=[pltpu.VMEM((SUBLANES, LANES), F32)],
        compiler_params=pltpu.CompilerParams(dimension_semantics=("arbitrary",),
                                             vmem_limit_bytes=VMEM_LIMIT),
        name="moe_router",
    )(xt, w_hi, w_lo, b)
    return idx[:TOP_K], gate, rank[:TOP_K], cnt[0, :N_EXPERTS], xp


def _sc_workers():
    mesh = plsc.VectorSubcoreMesh(core_axis_name="c", subcore_axis_name="s")
    return mesh, mesh.num_cores, mesh.num_cores * mesh.num_subcores


def _dispatch(xt, dest_km, n_rows):
    t = xt.shape[0]
    mesh, n_cores, n_workers = _sc_workers()
    n_ch = t // n_workers // SC_ROWS
    assert n_ch * SC_ROWS * n_workers == t and n_ch % 2 == 0

    def body(x_hbm, dest_hbm, xs_hbm, idx_v, rows_v, load_sem, scat_sem):
        wid = lax.axis_index("s") * n_cores + lax.axis_index("c")
        base = wid * (n_ch * SC_ROWS)
        for k in range(TOP_K):
            pltpu.sync_copy(dest_hbm.at[pl.ds(k * (t // SC_ROWS) + wid * n_ch, n_ch)], idx_v.at[k])

        def load(j, b):
            off = pl.multiple_of(base + j * SC_ROWS, SC_ROWS)
            return pltpu.make_async_copy(x_hbm.at[pl.ds(off, SC_ROWS)], rows_v.at[b], load_sem.at[b])

        def scatter(j, b, k):
            return pltpu.make_async_copy(rows_v.at[b], xs_hbm.at[idx_v.at[k, j]], scat_sem.at[b])

        load(0, 0).start()

        @pl.loop(0, n_ch, step=2)
        def _(j0):
            for b in range(2):
                j = j0 + b
                load(j, b).wait()

                @pl.when(j >= 1)
                def _():
                    for k in range(TOP_K):
                        scatter(j - 1, 1 - b, k).wait()

                @pl.when(j + 1 < n_ch)
                def _():
                    load(j + 1, 1 - b).start()

                for k in range(TOP_K):
                    scatter(j, b, k).start()

        for k in range(TOP_K):
            scatter(n_ch - 1, 1, k).wait()

    return pl.kernel(
        body, mesh=mesh,
        out_type=jax.ShapeDtypeStruct((n_rows, xt.shape[1]), xt.dtype),
        scratch_types=[pltpu.VMEM((TOP_K, n_ch, SC_ROWS), jnp.int32),
                       pltpu.VMEM((2, SC_ROWS, xt.shape[1]), xt.dtype),
                       pltpu.SemaphoreType.DMA((2,)), pltpu.SemaphoreType.DMA((2,))],
    )(xt, dest_km)


def _gather_rows(yb, dest_km):
    n_idx = dest_km.shape[0] * SC_ROWS
    mesh, n_cores, n_workers = _sc_workers()
    n_ch = n_idx // n_workers // SC_ROWS
    assert n_ch * SC_ROWS * n_workers == n_idx and n_ch % 2 == 0

    def body(yb_hbm, dest_hbm, out_hbm, idx_v, rows_v, gat_sem, store_sem):
        wid = lax.axis_index("s") * n_cores + lax.axis_index("c")
        base = wid * (n_ch * SC_ROWS)
        pltpu.sync_copy(dest_hbm.at[pl.ds(wid * n_ch, n_ch)], idx_v)

        def gather(j, b):
            return pltpu.make_async_copy(yb_hbm.at[idx_v.at[j]], rows_v.at[b], gat_sem.at[b])

        def store(j, b):
            off = pl.multiple_of(base + j * SC_ROWS, SC_ROWS)
            return pltpu.make_async_copy(rows_v.at[b], out_hbm.at[pl.ds(off, SC_ROWS)], store_sem.at[b])

        gather(0, 0).start()

        @pl.loop(0, n_ch, step=2)
        def _(j0):
            for b in range(2):
                j = j0 + b
                gather(j, b).wait()

                @pl.when(j >= 1)
                def _():
                    store(j - 1, 1 - b).wait()

                @pl.when(j + 1 < n_ch)
                def _():
                    gather(j + 1, 1 - b).start()

                store(j, b).start()

        store(n_ch - 1, 1).wait()

    return pl.kernel(
        body, mesh=mesh,
        out_type=jax.ShapeDtypeStruct((n_idx, yb.shape[1]), yb.dtype),
        scratch_types=[pltpu.VMEM((n_ch, SC_ROWS), jnp.int32),
                       pltpu.VMEM((2, SC_ROWS, yb.shape[1]), yb.dtype),
                       pltpu.SemaphoreType.DMA((2,)), pltpu.SemaphoreType.DMA((2,))],
    )(yb, dest_km)


def _expert_kernel(be_ref, nused_ref, valid_ref, x_ref, wgu_ref, bgu_ref, wd_ref, bd_ref, out_ref):
    del be_ref
    i = pl.program_id(0)

    @pl.when(i < nused_ref[0])
    def _():
        row = lax.broadcasted_iota(jnp.int32, x_ref.shape, 0)
        x_lo, x_hi = _unpack_halves(jnp.where(row < valid_ref[i], x_ref[...], jnp.uint32(0)))
        half = x_ref.shape[1]
        gu = (_dot(x_lo.astype(BF16), wgu_ref[0, 0, 0:half, :].astype(BF16))
              + _dot(x_hi.astype(BF16), wgu_ref[0, 0, half:, :].astype(BF16)) + bgu_ref[0, 0])
        d_exp = wd_ref.shape[2]
        gate = jnp.minimum(gu[:, :d_exp], SWIGLU_LIMIT)
        up = jnp.clip(gu[:, d_exp:], -SWIGLU_LIMIT, SWIGLU_LIMIT)
        act = (up + 1.0) * (gate * jax.nn.sigmoid(SWIGLU_ALPHA * gate))
        out_ref[...] = _pack_halves(_dot(act.astype(BF16), wd_ref[0, 0].astype(BF16)) + bd_ref[0, 0])

    @pl.when(i >= nused_ref[0])
    def _():
        out_ref[...] = jnp.zeros(out_ref.shape, jnp.uint32)


def _experts(xs, block_e, n_used, valid, layer, w_gu, b_gu, w_down, b_down):
    n_rows = xs.shape[0]
    n_blocks = n_rows // MOE_BLOCK
    depth, n_exp, d_exp = w_down.shape[:3]

    def of_expert(i, be, *_):
        return (layer, be[i], 0, 0)

    grid_spec = pltpu.PrefetchScalarGridSpec(
        num_scalar_prefetch=3,
        grid=(n_blocks,),
        in_specs=[
            pl.BlockSpec((MOE_BLOCK, D_MODEL // 2), lambda i, be, nu, *_: (jnp.minimum(i, nu[0] - 1), 0)),
            pl.BlockSpec((1, 1, D_MODEL, 2 * d_exp), of_expert),
            pl.BlockSpec((1, 1, 1, 2 * d_exp), of_expert),
            pl.BlockSpec((1, 1, d_exp, D_MODEL), of_expert),
            pl.BlockSpec((1, 1, 1, D_MODEL), of_expert),
        ],
        out_specs=pl.BlockSpec((MOE_BLOCK, D_MODEL // 2), lambda i, *_: (i, 0)),
    )
    return pl.pallas_call(
        _expert_kernel,
        grid_spec=grid_spec,
        out_shape=jax.ShapeDtypeStruct((n_rows, D_MODEL // 2), jnp.uint32),
        compiler_params=pltpu.CompilerParams(dimension_semantics=("arbitrary",),
                                             vmem_limit_bytes=VMEM_LIMIT),
        name="moe_experts",
    )(block_e, n_used, valid, xs, w_gu, b_gu.reshape(depth, n_exp, 1, -1), w_down,
      b_down.reshape(depth, n_exp, 1, -1))


def _combine_kernel(h_ref, gate_ref, y4_ref, lng_ref, lnb_ref, out_ref):
    gates = gate_ref[...]
    y_lo, y_hi = None, None
    for k in range(TOP_K):
        lo, hi = _unpack_halves(y4_ref[k])
        g = gates[:, k:k + 1]
        y_lo = g * lo if y_lo is None else y_lo + g * lo
        y_hi = g * hi if y_hi is None else y_hi + g * hi
    y = jnp.concatenate([y_lo, y_hi], axis=1)
    out_ref[...] = _layer_norm(ALPHA * h_ref[...] + y, lng_ref[...], lnb_ref[...])


def _combine(xt, gates, y4, ln_g, ln_b):
    t = xt.shape[0]
    rows = min(COMBINE_ROWS, t)
    tok_spec = pl.BlockSpec((rows, D_MODEL), lambda i: (i, 0))
    return pl.pallas_call(
        _combine_kernel,
        grid=(t // rows,),
        in_specs=[tok_spec,
                  pl.BlockSpec((rows, LANES), lambda i: (i, 0)),
                  pl.BlockSpec((TOP_K, rows, D_MODEL // 2), lambda i: (0, i, 0)),
                  _const_spec((1, D_MODEL)), _const_spec((1, D_MODEL))],
        out_specs=tok_spec,
        out_shape=jax.ShapeDtypeStruct((t, D_MODEL), F32),
        compiler_params=pltpu.CompilerParams(dimension_semantics=("arbitrary",),
                                             vmem_limit_bytes=VMEM_LIMIT),
        name="moe_combine",
    )(xt, gates, y4, ln_g.reshape(1, -1), ln_b.reshape(1, -1))


def _moe_layer(h, layer, w_r, b_r, w_gu, b_gu, w_down, b_down, ln_g, ln_b):
    bsz, seq, d = h.shape
    t = bsz * seq
    xt = h.reshape(t, d)
    idx, gates, rank, counts, xp = _router(xt, w_r, b_r)
    experts = jnp.arange(N_EXPERTS, dtype=jnp.int32)
    padded = ((counts + MOE_BLOCK - 1) // MOE_BLOCK) * MOE_BLOCK
    pad_ends = jnp.sum(jnp.where(experts[None, :] <= experts[:, None], padded[None, :], 0), axis=1)
    pad_starts = pad_ends - padded
    start_of = jnp.zeros_like(idx)
    for e in range(N_EXPERTS):
        start_of = jnp.where(idx == e, pad_starts[e], start_of)
    dest_km = (start_of + rank).reshape(t * TOP_K // SC_ROWS, SC_ROWS)
    n_blocks = -(-(t * TOP_K) // MOE_BLOCK) + N_EXPERTS
    block_lo = jnp.arange(n_blocks, dtype=jnp.int32) * MOE_BLOCK
    block_e = jnp.minimum(jnp.sum(pad_ends[None, :] <= block_lo[:, None], axis=1),
                          N_EXPERTS - 1).astype(jnp.int32)
    n_used = jnp.sum(padded, keepdims=True).astype(jnp.int32) // MOE_BLOCK
    of_block = block_e[:, None] == experts[None, :]
    rows_left = jnp.sum(jnp.where(of_block, (counts + pad_starts)[None, :], 0), axis=1) - block_lo
    valid = jnp.clip(rows_left, 0, MOE_BLOCK).astype(jnp.int32)
    xs = _dispatch(xp, dest_km, n_blocks * MOE_BLOCK)
    yb = _experts(xs, block_e, n_used, valid, layer, w_gu, b_gu, w_down, b_down)
    y4 = _gather_rows(yb, dest_km).reshape(TOP_K, t, d // 2)
    out = _combine(xt, gates, y4, ln_g, ln_b)
    return out.reshape(bsz, seq, d)


def kernel(x, mix_w_in, sc_conv_w, ssm_conv_w, ssm_conv_b, ssm_dt_bias, ssm_a_log, ssm_d, ssm_norm_w, mix_w_out, conf_w_pw1, conf_b_pw1, conf_w_dw, conf_b_dw, conf_ln_g, conf_ln_b, conf_w_pw2, conf_b_pw2, router_w, router_b, exp_w_gu, exp_b_gu, exp_w_down, exp_b_down, ln_mix_g, ln_mix_b, ln_ffn_g, ln_ffn_b):
    h = x
    for i in range(DEPTH):
        j = i // 2
        if i % 2 == 0:
            h = _even_mixer(h, mix_w_in[j], sc_conv_w[j], ssm_conv_w[j], ssm_conv_b[j],
                            ssm_dt_bias[j], ssm_a_log[j], ssm_d[j], ssm_norm_w[j], mix_w_out[j],
                            ln_mix_g[i], ln_mix_b[i])
        else:
            h = _odd_mixer(h, conf_w_pw1[j], conf_b_pw1[j], conf_w_dw[j], conf_b_dw[j],
                           conf_ln_g[j], conf_ln_b[j], conf_w_pw2[j], conf_b_pw2[j],
                           ln_mix_g[i], ln_mix_b[i])
        h = _moe_layer(h, i, router_w[i], router_b[i], exp_w_gu, exp_b_gu, exp_w_down,
                       exp_b_down, ln_ffn_g[i], ln_ffn_b[i])
    return h
```

```python
import functools
import math

import jax
import jax.numpy as jnp
from jax import lax
from jax.experimental import pallas as pl
from jax.experimental.pallas import tpu as pltpu
from jax.experimental.pallas import tpu_sc as plsc

F32 = jnp.float32
BF16 = jnp.bfloat16

D_MODEL = 1024
DEPTH = 4
ALPHA = (2 * DEPTH) ** 0.25
LN_EPS = 1e-5
SC_KERNEL = 3
SSM_HEAD_DIM = 64
SSM_HEADS = 16
SSM_GROUPS = 2
SSM_STATE = 128
SSM_CONV = 4
SSM_CHUNK = 128
SSM_INNER = D_MODEL
GROUP_COLS = SSM_INNER // SSM_GROUPS
XBC_COLS = SSM_INNER + 2 * SSM_GROUPS * SSM_STATE
CONF_KERNEL = 31
N_EXPERTS = 32
TOP_K = 4
SWIGLU_LIMIT = 7.0
SWIGLU_ALPHA = 1.702
MOE_BLOCK = 512

LANES = 128
SUBLANES = 8
CARRY_ROWS = 8
CONF_CARRY_ROWS = 32
CONF_TILE_ROWS = 32
VMEM_LIMIT = 56 * 1024 * 1024

MIX_ROWS = 256
SSD_MIX_ROWS = 512
ROUTER_ROWS = 512
COMBINE_ROWS = 256
COMBINE_PARTS = 2
SC_ROWS = 64


def _layer_norm(v, g, b):
    mu = jnp.mean(v, axis=-1, keepdims=True)
    d = v - mu
    var = jnp.mean(d * d, axis=-1, keepdims=True)
    return d * lax.rsqrt(var + LN_EPS) * g + b


def _silu(v):
    return v * jax.nn.sigmoid(v)


def _softplus(v):
    return jnp.maximum(v, 0.0) + jnp.log1p(jnp.exp(-jnp.abs(v)))


def _dot(a, b):
    return jnp.dot(a, b, preferred_element_type=F32)


def _dot_exact01(m01, v):
    v1 = v.astype(BF16)
    r1 = v - v1.astype(F32)
    v2 = r1.astype(BF16)
    v3 = (r1 - v2.astype(F32)).astype(BF16)
    return _dot(m01, v1) + _dot(m01, v2) + _dot(m01, v3)


def _even_mixer_kernel(x_ref, w_sc_ref, w_z_ref, w_xbc_ref, w_dt_ref, scw_ref, ssmw_ref,
                       ssmb_ref, dtb_ref, a_ref, dexp_ref, normw_ref, wout_ref,
                       lng_ref, lnb_ref, out_ref,
                       u_buf, xbc_buf, act_buf, dtn_buf, y_buf, state_ref):
    rows = x_ref.shape[1]
    c = pl.program_id(1)

    @pl.when(c == 0)
    def _():
        u_buf[0:CARRY_ROWS, :] = jnp.zeros((CARRY_ROWS, D_MODEL), F32)
        xbc_buf[0:CARRY_ROWS, :] = jnp.zeros((CARRY_ROWS, XBC_COLS), F32)
        state_ref[...] = jnp.zeros(state_ref.shape, F32)

    x = x_ref[0]
    xb = x.astype(BF16)

    p = _dot(xb, w_sc_ref[...])
    u = p[:, D_MODEL:2 * D_MODEL] * p[:, :D_MODEL]
    u_buf[CARRY_ROWS:CARRY_ROWS + rows, :] = u
    conv = scw_ref[2:3, :] * u
    for k in range(SC_KERNEL - 1):
        shift = SC_KERNEL - 1 - k
        conv = conv + scw_ref[k:k + 1, :] * u_buf[CARRY_ROWS - shift:CARRY_ROWS - shift + rows, :]
    y_sc = p[:, 2 * D_MODEL:] * conv
    u_buf[0:CARRY_ROWS, :] = u[rows - CARRY_ROWS:, :]

    xbc = _dot(xb, w_xbc_ref[...])
    xbc_buf[CARRY_ROWS:CARRY_ROWS + rows, :] = xbc
    cv = ssmw_ref[SSM_CONV - 1:SSM_CONV, :] * xbc + ssmb_ref[...]
    for k in range(SSM_CONV - 1):
        shift = SSM_CONV - 1 - k
        cv = cv + ssmw_ref[k:k + 1, :] * xbc_buf[CARRY_ROWS - shift:CARRY_ROWS - shift + rows, :]
    act_buf[...] = _silu(cv)
    xbc_buf[0:CARRY_ROWS, :] = xbc[rows - CARRY_ROWS:, :]

    dtn_buf[...] = _softplus(_dot(xb, w_dt_ref[...]) + dtb_ref[...])

    li = lax.broadcasted_iota(jnp.int32, (SSM_CHUNK, SSM_CHUNK), 0)
    si = lax.broadcasted_iota(jnp.int32, (SSM_CHUNK, SSM_CHUNK), 1)
    causal = li >= si
    tril = jnp.where(causal, 1.0, 0.0).astype(BF16)
    low_half = si < SSM_HEAD_DIM
    low_row = low_half[0:1, :]

    for j in range(rows // SSM_CHUNK):
        r0 = j * SSM_CHUNK
        dt_c = dtn_buf[r0:r0 + SSM_CHUNK, :]
        acum = _dot_exact01(tril, dt_c * a_ref[...])
        alast = acum[SSM_CHUNK - 1:SSM_CHUNK, :]
        acum_t = acum.T
        dt_t = dt_c.T
        e_acum = jnp.exp(acum)
        w_t = (jnp.exp(alast - acum) * dt_c).T
        chunk_decay = jnp.exp(alast)
        xs_b = act_buf[r0:r0 + SSM_CHUNK, 0:SSM_INNER].astype(BF16)
        for g in range(SSM_GROUPS):
            b_g = act_buf[r0:r0 + SSM_CHUNK, SSM_INNER + g * SSM_STATE:SSM_INNER + (g + 1) * SSM_STATE]
            c_lo = SSM_INNER + (SSM_GROUPS + g) * SSM_STATE
            c_g = act_buf[r0:r0 + SSM_CHUNK, c_lo:c_lo + SSM_STATE]
            cb = lax.dot_general(c_g.astype(BF16), b_g.astype(BF16), (((1,), (1,)), ((), ())),
                                 preferred_element_type=F32)
            b_gt = b_g.T
            for pair in range(GROUP_COLS // LANES):
                col0 = g * GROUP_COLS + pair * LANES
                h0 = col0 // SSM_HEAD_DIM
                xs_p = xs_b[:, col0:col0 + LANES]
                zero_b = jnp.zeros_like(xs_p)
                xs_lo = jnp.where(low_half, xs_p, zero_b)
                xs_hi = jnp.where(low_half, zero_b, xs_p)
                st = state_ref[g, :, pair * LANES:(pair + 1) * LANES]
                st_b = st.astype(BF16)
                lhs, lhs_s = [], []
                for h in (h0, h0 + 1):
                    seg = acum[:, h:h + 1] - acum_t[h:h + 1, :]
                    decay = jnp.exp(jnp.where(causal, seg, -jnp.inf))
                    lhs.append((cb * decay * dt_t[h:h + 1, :]).astype(BF16))
                    lhs_s.append((b_gt * w_t[h:h + 1, :]).astype(BF16))
                for h in (h0, h0 + 1):
                    lhs.append((c_g * e_acum[:, h:h + 1]).astype(BF16))
                rhs = jnp.concatenate([xs_lo, xs_hi, jnp.where(low_half, st_b, zero_b),
                                       jnp.where(low_half, zero_b, st_b)], axis=0)
                y_buf[r0:r0 + SSM_CHUNK, col0:col0 + LANES] = _dot(
                    jnp.concatenate(lhs, axis=1), rhs)
                cd_p = jnp.where(low_row, chunk_decay[:, h0:h0 + 1], chunk_decay[:, h0 + 1:h0 + 2])
                state_ref[g, :, pair * LANES:(pair + 1) * LANES] = st * cd_p + _dot(
                    jnp.concatenate(lhs_s, axis=1), jnp.concatenate([xs_lo, xs_hi], axis=0))

    z = _dot(xb, w_z_ref[...])
    y = (y_buf[...] + act_buf[:, 0:SSM_INNER] * dexp_ref[...]) * _silu(z)
    parts = []
    for g in range(SSM_GROUPS):
        yg = y[:, g * GROUP_COLS:(g + 1) * GROUP_COLS]
        ms = jnp.mean(yg * yg, axis=-1, keepdims=True)
        parts.append(yg * lax.rsqrt(ms + LN_EPS) * normw_ref[:, g * GROUP_COLS:(g + 1) * GROUP_COLS])
    mix = _dot(y_sc.astype(BF16), wout_ref[0:D_MODEL, :])
    for g in range(SSM_GROUPS):
        lo = D_MODEL + g * GROUP_COLS
        mix = mix + _dot(parts[g].astype(BF16), wout_ref[lo:lo + GROUP_COLS, :])
    out_ref[0] = _layer_norm(ALPHA * x + mix, lng_ref[...], lnb_ref[...])


def _const_spec(shape):
    nd = len(shape)
    return pl.BlockSpec(shape, lambda *_: (0,) * nd, pipeline_mode=pl.Buffered(1))


def _even_mixer(h, w_in, sc_conv_w, ssm_conv_w, ssm_conv_b, dt_bias, a_log, d_skip, norm_w,
                w_out, ln_g, ln_b):
    bsz, seq, _ = h.shape
    rows = min(SSD_MIX_ROWS, seq)
    cut_z = 3 * D_MODEL
    cut_xbc = cut_z + SSM_INNER
    cut_dt = cut_xbc + XBC_COLS
    w_sc = w_in[:, :cut_z].astype(BF16)
    w_z = w_in[:, cut_z:cut_xbc].astype(BF16)
    w_xbc = w_in[:, cut_xbc:cut_dt].astype(BF16)
    pad = LANES - SSM_HEADS
    w_dt = jnp.pad(w_in[:, cut_dt:], ((0, 0), (0, pad))).astype(BF16)
    dtb = jnp.pad(dt_bias, (0, pad)).reshape(1, LANES)
    a = -jnp.exp(a_log)
    a_n = jnp.pad(a, (0, pad)).reshape(1, LANES)
    d_exp = jnp.repeat(d_skip, SSM_HEAD_DIM).reshape(1, SSM_INNER)
    consts = [w_sc, w_z, w_xbc, w_dt, sc_conv_w, ssm_conv_w, ssm_conv_b.reshape(1, -1), dtb, a_n,
              d_exp, norm_w.reshape(1, -1), w_out.astype(BF16), ln_g.reshape(1, -1),
              ln_b.reshape(1, -1)]
    x_spec = pl.BlockSpec((1, rows, D_MODEL), lambda b, c: (b, c, 0))
    return pl.pallas_call(
        _even_mixer_kernel,
        grid=(bsz, seq // rows),
        in_specs=[x_spec] + [_const_spec(v.shape) for v in consts],
        out_specs=x_spec,
        out_shape=jax.ShapeDtypeStruct(h.shape, F32),
        scratch_shapes=[
            pltpu.VMEM((CARRY_ROWS + rows, D_MODEL), F32),
            pltpu.VMEM((CARRY_ROWS + rows, XBC_COLS), F32),
            pltpu.VMEM((rows, XBC_COLS), F32),
            pltpu.VMEM((rows, LANES), F32),
            pltpu.VMEM((rows, SSM_INNER), F32),
            pltpu.VMEM((SSM_GROUPS, SSM_STATE, GROUP_COLS), F32),
        ],
        compiler_params=pltpu.CompilerParams(
            dimension_semantics=("arbitrary", "arbitrary"), vmem_limit_bytes=VMEM_LIMIT),
        name="even_mixer",
    )(h, *consts)


def _odd_mixer_kernel(x_ref, w1_ref, b1_ref, wdw_ref, bdw_ref, g_ref, b_ref, w2_ref, b2_ref,
                      lng_ref, lnb_ref, out_ref, sh_buf, conv_buf):
    rows = x_ref.shape[1]
    c = pl.program_id(1)

    @pl.when(c == 0)
    def _():
        sh_buf[0, 0:CONF_CARRY_ROWS, :] = jnp.zeros((CONF_CARRY_ROWS, D_MODEL), F32)

    x = x_ref[0]
    u = _dot(x.astype(BF16), w1_ref[...]) + b1_ref[...]
    u = u[:, :D_MODEL] * jax.nn.sigmoid(u[:, D_MODEL:])
    sh_buf[0, CONF_CARRY_ROWS:CONF_CARRY_ROWS + rows, :] = u
    span = rows + CONF_CARRY_ROWS - SUBLANES
    for r in range(1, SUBLANES):
        sh_buf[r, 0:span, :] = sh_buf[0, r:r + span, :]

    def conv_tile(i, carry):
        base = pl.multiple_of(i * CONF_TILE_ROWS, CONF_TILE_ROWS)
        groups = CONF_TILE_ROWS // SUBLANES
        acc = [bdw_ref[...] for _ in range(groups)]
        for k in range(CONF_KERNEL):
            lo = CONF_CARRY_ROWS - (CONF_KERNEL - 1 - k)
            w_k = wdw_ref[k]
            for g in range(groups):
                row0 = base + (lo // SUBLANES + g) * SUBLANES
                acc[g] = acc[g] + w_k * sh_buf[lo % SUBLANES, pl.ds(row0, SUBLANES), :]
        for g in range(groups):
            conv_buf[pl.ds(base + g * SUBLANES, SUBLANES), :] = acc[g]
        return carry

    lax.fori_loop(0, rows // CONF_TILE_ROWS, conv_tile, 0)
    sh_buf[0, 0:CONF_CARRY_ROWS, :] = u[rows - CONF_CARRY_ROWS:, :]
    v = _silu(_layer_norm(conv_buf[...], g_ref[...], b_ref[...]))
    mix = _dot(v.astype(BF16), w2_ref[...]) + b2_ref[...]
    out_ref[0] = _layer_norm(ALPHA * x + mix, lng_ref[...], lnb_ref[...])


def _odd_mixer(h, w_pw1, b_pw1, w_dw, b_dw, ln_g_c, ln_b_c, w_pw2, b_pw2, ln_g, ln_b):
    bsz, seq, _ = h.shape
    rows = min(MIX_ROWS, seq)
    w_dw_rows = jnp.broadcast_to(w_dw[:, None, :], (CONF_KERNEL, SUBLANES, D_MODEL))
    b_dw_rows = jnp.broadcast_to(b_dw[None, :], (SUBLANES, D_MODEL))
    consts = [w_pw1.astype(BF16), b_pw1.reshape(1, -1), w_dw_rows, b_dw_rows,
              ln_g_c.reshape(1, -1), ln_b_c.reshape(1, -1), w_pw2.astype(BF16),
              b_pw2.reshape(1, -1), ln_g.reshape(1, -1), ln_b.reshape(1, -1)]
    x_spec = pl.BlockSpec((1, rows, D_MODEL), lambda b, c: (b, c, 0))
    return pl.pallas_call(
        _odd_mixer_kernel,
        grid=(bsz, seq // rows),
        in_specs=[x_spec] + [_const_spec(v.shape) for v in consts],
        out_specs=x_spec,
        out_shape=jax.ShapeDtypeStruct(h.shape, F32),
        scratch_shapes=[pltpu.VMEM((SUBLANES, CONF_CARRY_ROWS + rows, D_MODEL), F32),
                        pltpu.VMEM((rows, D_MODEL), F32)],
        compiler_params=pltpu.CompilerParams(
            dimension_semantics=("arbitrary", "arbitrary"), vmem_limit_bytes=VMEM_LIMIT),
        name="odd_mixer",
    )(h, *consts)


def _pack_halves(v):
    n = v.shape[1] // 2
    lo = lax.bitcast_convert_type(v[:, :n].astype(BF16).astype(F32), jnp.uint32)
    hi = lax.bitcast_convert_type(v[:, n:].astype(BF16).astype(F32), jnp.uint32)
    return (lo >> 16) | (hi & jnp.uint32(0xFFFF0000))


def _unpack_halves(w):
    lo = lax.bitcast_convert_type(w << 16, F32)
    hi = lax.bitcast_convert_type(w & jnp.uint32(0xFFFF0000), F32)
    return lo, hi


def _router_kernel(x_ref, w_hi_ref, w_lo_ref, b_ref, idx_ref, gate_ref, rank_ref, cnt_ref, xp_ref,
                   carry_ref):
    rows = x_ref.shape[0]
    i = pl.program_id(0)

    @pl.when(i == 0)
    def _():
        carry_ref[...] = jnp.zeros(carry_ref.shape, F32)

    x = x_ref[...]
    x1 = x.astype(BF16)
    x2 = (x - x1.astype(F32)).astype(BF16)
    logits = (_dot(x1, w_hi_ref[...]) + (_dot(x1, w_lo_ref[...]) + _dot(x2, w_hi_ref[...]))
              + b_ref[...])
    xp_ref[...] = _pack_halves(x)
    lane = lax.broadcasted_iota(jnp.int32, (rows, LANES), 1)
    lane_f = lane.astype(F32)
    work = logits
    vals, sels, idxs = [], [], []
    for _ in range(TOP_K):
        m = jnp.max(work, axis=-1, keepdims=True)
        first = jnp.min(jnp.where(work == m, lane_f, float(LANES)), axis=-1, keepdims=True)
        sel = lane_f == first
        work = jnp.where(sel, -jnp.inf, work)
        vals.append(m)
        sels.append(sel)
        idxs.append(first)
    exps = [jnp.exp(v - vals[0]) for v in vals]
    inv = 1.0 / (exps[0] + exps[1] + exps[2] + exps[3])
    onehot = jnp.zeros((rows, LANES), F32)
    for sel in sels:
        onehot = onehot + jnp.where(sel, 1.0, 0.0)
    ri = lax.broadcasted_iota(jnp.int32, (rows, rows), 0)
    ci = lax.broadcasted_iota(jnp.int32, (rows, rows), 1)
    strict = jnp.where(ri > ci, 1.0, 0.0).astype(BF16)
    before = _dot(strict, onehot.astype(BF16)) + carry_ref[0:1, :]
    idx_o = jnp.zeros((rows, LANES), F32)
    gate_o = jnp.zeros((rows, LANES), F32)
    rank_o = jnp.zeros((rows, LANES), F32)
    for k in range(TOP_K):
        rank_k = jnp.sum(jnp.where(sels[k], before, 0.0), axis=-1, keepdims=True)
        here = lane == k
        idx_o = jnp.where(here, idxs[k], idx_o)
        gate_o = jnp.where(here, exps[k] * inv, gate_o)
        rank_o = jnp.where(here, rank_k, rank_o)
    gate_ref[...] = gate_o
    idx_ref[...] = idx_o.T[0:SUBLANES, :].astype(jnp.int32)
    rank_ref[...] = rank_o.T[0:SUBLANES, :].astype(jnp.int32)
    total = carry_ref[0:1, :] + jnp.sum(onehot, axis=0, keepdims=True)
    carry_ref[...] = jnp.broadcast_to(total, carry_ref.shape)
    cnt_ref[...] = jnp.broadcast_to(total, cnt_ref.shape).astype(jnp.int32)


def _router(xt, w_r, b_r):
    t = xt.shape[0]
    rows = min(ROUTER_ROWS, t)
    pad = LANES - N_EXPERTS
    w = jnp.pad(w_r, ((0, 0), (0, pad)))
    w_hi = w.astype(BF16)
    w_lo = (w - w_hi.astype(F32)).astype(BF16)
    b = jnp.pad(b_r, (0, pad), constant_values=-1e30).reshape(1, LANES)
    km_spec = pl.BlockSpec((SUBLANES, rows), lambda i: (0, i))
    idx, gate, rank, cnt, xp = pl.pallas_call(
        _router_kernel,
        grid=(t // rows,),
        in_specs=[pl.BlockSpec((rows, D_MODEL), lambda i: (i, 0)), _const_spec(w.shape),
                  _const_spec(w.shape), _const_spec(b.shape)],
        out_specs=[km_spec, pl.BlockSpec((rows, LANES), lambda i: (i, 0)), km_spec,
                   pl.BlockSpec((SUBLANES, LANES), lambda i: (0, 0)),
                   pl.BlockSpec((rows, D_MODEL // 2), lambda i: (i, 0))],
        out_shape=[jax.ShapeDtypeStruct((SUBLANES, t), jnp.int32),
                   jax.ShapeDtypeStruct((t, LANES), F32),
                   jax.ShapeDtypeStruct((SUBLANES, t), jnp.int32),
                   jax.ShapeDtypeStruct((SUBLANES, LANES), jnp.int32),
                   jax.ShapeDtypeStruct((t, D_MODEL // 2), jnp.uint32)],
        scratch_shapes=[pltpu.VMEM((SUBLANES, LANES), F32)],
        compiler_params=pltpu.CompilerParams(dimension_semantics=("arbitrary",),
                                             vmem_limit_bytes=VMEM_LIMIT),
        name="moe_router",
    )(xt, w_hi, w_lo, b)
    return idx, gate, rank, cnt[0, :N_EXPERTS], xp


def _dest_kernel(starts_ref, idx_ref, rank_ref, dest_ref):
    acc = rank_ref[...]
    idx = idx_ref[...]
    for e in range(N_EXPERTS):
        acc = acc + jnp.where(idx == e, starts_ref[e], 0)
    dest_ref[...] = acc


def _dest_rows(pad_starts, idx, rank):
    return pl.pallas_call(
        _dest_kernel,
        in_specs=[pl.BlockSpec(memory_space=pltpu.SMEM), pl.BlockSpec(idx.shape, lambda: (0, 0)),
                  pl.BlockSpec(idx.shape, lambda: (0, 0))],
        out_specs=pl.BlockSpec(idx.shape, lambda: (0, 0)),
        out_shape=jax.ShapeDtypeStruct(idx.shape, jnp.int32),
        name="moe_dest",
    )(pad_starts, idx, rank)


def _sc_workers():
    mesh = plsc.VectorSubcoreMesh(core_axis_name="c", subcore_axis_name="s")
    return mesh, mesh.num_cores, mesh.num_cores * mesh.num_subcores


def _dispatch(xt, dest_km, n_rows):
    t = xt.shape[0]
    mesh, n_cores, n_workers = _sc_workers()
    n_ch = t // n_workers // SC_ROWS
    assert n_ch * SC_ROWS * n_workers == t and n_ch % 2 == 0

    def body(x_hbm, dest_hbm, xs_hbm, idx_v, rows_v, load_sem, scat_sem):
        wid = lax.axis_index("s") * n_cores + lax.axis_index("c")
        base = wid * (n_ch * SC_ROWS)
        for k in range(TOP_K):
            pltpu.sync_copy(dest_hbm.at[pl.ds(k * (t // SC_ROWS) + wid * n_ch, n_ch)], idx_v.at[k])

        def load(j, b):
            off = pl.multiple_of(base + j * SC_ROWS, SC_ROWS)
            return pltpu.make_async_copy(x_hbm.at[pl.ds(off, SC_ROWS)], rows_v.at[b], load_sem.at[b])

        def scatter(j, b, k):
            return pltpu.make_async_copy(rows_v.at[b], xs_hbm.at[idx_v.at[k, j]], scat_sem.at[b])

        load(0, 0).start()

        @pl.loop(0, n_ch, step=2)
        def _(j0):
            for b in range(2):
                j = j0 + b
                load(j, b).wait()

                @pl.when(j >= 1)
                def _():
                    for k in range(TOP_K):
                        scatter(j - 1, 1 - b, k).wait()

                @pl.when(j + 1 < n_ch)
                def _():
                    load(j + 1, 1 - b).start()

                for k in range(TOP_K):
                    scatter(j, b, k).start()

        for k in range(TOP_K):
            scatter(n_ch - 1, 1, k).wait()

    return pl.kernel(
        body, mesh=mesh,
        out_type=jax.ShapeDtypeStruct((n_rows, xt.shape[1]), xt.dtype),
        scratch_types=[pltpu.VMEM((TOP_K, n_ch, SC_ROWS), jnp.int32),
                       pltpu.VMEM((2, SC_ROWS, xt.shape[1]), xt.dtype),
                       pltpu.SemaphoreType.DMA((2,)), pltpu.SemaphoreType.DMA((2,))],
    )(xt, dest_km)


def _gather_rows(yb, dest_km):
    n_idx = dest_km.shape[0] * SC_ROWS
    mesh, n_cores, n_workers = _sc_workers()
    n_ch = n_idx // n_workers // SC_ROWS
    assert n_ch * SC_ROWS * n_workers == n_idx and n_ch % 2 == 0

    def body(yb_hbm, dest_hbm, out_hbm, idx_v, rows_v, gat_sem, store_sem):
        wid = lax.axis_index("s") * n_cores + lax.axis_index("c")
        base = wid * (n_ch * SC_ROWS)
        pltpu.sync_copy(dest_hbm.at[pl.ds(wid * n_ch, n_ch)], idx_v)

        def gather(j, b):
            return pltpu.make_async_copy(yb_hbm.at[idx_v.at[j]], rows_v.at[b], gat_sem.at[b])

        def store(j, b):
            off = pl.multiple_of(base + j * SC_ROWS, SC_ROWS)
            return pltpu.make_async_copy(rows_v.at[b], out_hbm.at[pl.ds(off, SC_ROWS)], store_sem.at[b])

        gather(0, 0).start()

        @pl.loop(0, n_ch, step=2)
        def _(j0):
            for b in range(2):
                j = j0 + b
                gather(j, b).wait()

                @pl.when(j >= 1)
                def _():
                    store(j - 1, 1 - b).wait()

                @pl.when(j + 1 < n_ch)
                def _():
                    gather(j + 1, 1 - b).start()

                store(j, b).start()

        store(n_ch - 1, 1).wait()

    return pl.kernel(
        body, mesh=mesh,
        out_type=jax.ShapeDtypeStruct((n_idx, yb.shape[1]), yb.dtype),
        scratch_types=[pltpu.VMEM((n_ch, SC_ROWS), jnp.int32),
                       pltpu.VMEM((2, SC_ROWS, yb.shape[1]), yb.dtype),
                       pltpu.SemaphoreType.DMA((2,)), pltpu.SemaphoreType.DMA((2,))],
    )(yb, dest_km)


def _expert_kernel(be_ref, nused_ref, valid_ref, x_ref, wgu_ref, bgu_ref, wd_ref, bd_ref, out_ref):
    del be_ref
    i = pl.program_id(0)

    @pl.when(i < nused_ref[0])
    def _():
        row = lax.broadcasted_iota(jnp.int32, x_ref.shape, 0)
        x_lo, x_hi = _unpack_halves(jnp.where(row < valid_ref[i], x_ref[...], jnp.uint32(0)))
        half = x_ref.shape[1]
        gu = (_dot(x_lo.astype(BF16), wgu_ref[0, 0, 0:half, :].astype(BF16))
              + _dot(x_hi.astype(BF16), wgu_ref[0, 0, half:, :].astype(BF16)) + bgu_ref[0, 0])
        d_exp = wd_ref.shape[2]
        gate = jnp.minimum(gu[:, :d_exp], SWIGLU_LIMIT)
        up = jnp.clip(gu[:, d_exp:], -SWIGLU_LIMIT, SWIGLU_LIMIT)
        act = (up + 1.0) * (gate * jax.nn.sigmoid(SWIGLU_ALPHA * gate))
        out_ref[...] = _pack_halves(_dot(act.astype(BF16), wd_ref[0, 0].astype(BF16)) + bd_ref[0, 0])

    @pl.when(i >= nused_ref[0])
    def _():
        out_ref[...] = jnp.zeros(out_ref.shape, jnp.uint32)


def _experts(xs, block_e, n_used, valid, layer, w_gu, b_gu, w_down, b_down):
    n_rows = xs.shape[0]
    n_blocks = n_rows // MOE_BLOCK
    depth, n_exp, d_exp = w_down.shape[:3]

    def of_expert(i, be, *_):
        return (layer, be[i], 0, 0)

    grid_spec = pltpu.PrefetchScalarGridSpec(
        num_scalar_prefetch=3,
        grid=(n_blocks,),
        in_specs=[
            pl.BlockSpec((MOE_BLOCK, D_MODEL // 2), lambda i, be, nu, *_: (jnp.minimum(i, nu[0] - 1), 0)),
            pl.BlockSpec((1, 1, D_MODEL, 2 * d_exp), of_expert),
            pl.BlockSpec((1, 1, 1, 2 * d_exp), of_expert),
            pl.BlockSpec((1, 1, d_exp, D_MODEL), of_expert),
            pl.BlockSpec((1, 1, 1, D_MODEL), of_expert),
        ],
        out_specs=pl.BlockSpec((MOE_BLOCK, D_MODEL // 2), lambda i, *_: (i, 0)),
    )
    return pl.pallas_call(
        _expert_kernel,
        grid_spec=grid_spec,
        out_shape=jax.ShapeDtypeStruct((n_rows, D_MODEL // 2), jnp.uint32),
        compiler_params=pltpu.CompilerParams(dimension_semantics=("arbitrary",),
                                             vmem_limit_bytes=VMEM_LIMIT),
        name="moe_experts",
    )(block_e, n_used, valid, xs, w_gu, b_gu.reshape(depth, n_exp, 1, -1), w_down,
      b_down.reshape(depth, n_exp, 1, -1))


def _combine_kernel(h_ref, gate_ref, y4_ref, lng_ref, lnb_ref, *rest):
    out_ref = rest[-1]
    gates = gate_ref[...]
    y_lo, y_hi = None, None
    for k in range(TOP_K):
        lo, hi = _unpack_halves(y4_ref[k])
        g = gates[:, k:k + 1]
        y_lo = g * lo if y_lo is None else y_lo + g * lo
        y_hi = g * hi if y_hi is None else y_hi + g * hi
    y = jnp.concatenate([y_lo, y_hi], axis=1)
    out_ref[...] = _layer_norm(ALPHA * h_ref[...] + y, lng_ref[...], lnb_ref[...])


def _combine(xt, gates, y4, part, earlier, ln_g, ln_b):
    t = xt.shape[0]
    tp = y4.shape[1]
    rows = min(COMBINE_ROWS, tp)
    first = part * (tp // rows)
    tok_spec = pl.BlockSpec((rows, D_MODEL), lambda i: (first + i, 0))
    in_specs = [tok_spec,
                pl.BlockSpec((rows, LANES), lambda i: (first + i, 0)),
                pl.BlockSpec((TOP_K, rows, D_MODEL // 2), lambda i: (0, i, 0)),
                _const_spec((1, D_MODEL)), _const_spec((1, D_MODEL))]
    args = [xt, gates, y4, ln_g.reshape(1, -1), ln_b.reshape(1, -1)]
    aliases = {}
    if earlier is not None:
        in_specs.append(pl.BlockSpec(memory_space=pl.ANY))
        args.append(earlier)
        aliases = {len(args) - 1: 0}
    return pl.pallas_call(
        _combine_kernel,
        grid=(tp // rows,),
        in_specs=in_specs,
        out_specs=tok_spec,
        out_shape=jax.ShapeDtypeStruct((t, D_MODEL), F32),
        input_output_aliases=aliases,
        compiler_params=pltpu.CompilerParams(dimension_semantics=("arbitrary",),
                                             vmem_limit_bytes=VMEM_LIMIT),
        name="moe_combine",
    )(*args)


def _moe_layer(h, layer, w_r, b_r, w_gu, b_gu, w_down, b_down, ln_g, ln_b):
    bsz, seq, d = h.shape
    t = bsz * seq
    xt = h.reshape(t, d)
    idx, gates, rank, counts, xp = _router(xt, w_r, b_r)
    experts = jnp.arange(N_EXPERTS, dtype=jnp.int32)
    padded = ((counts + MOE_BLOCK - 1) // MOE_BLOCK) * MOE_BLOCK
    pad_ends = jnp.sum(jnp.where(experts[None, :] <= experts[:, None], padded[None, :], 0), axis=1)
    pad_starts = pad_ends - padded
    dest = _dest_rows(pad_starts.astype(jnp.int32), idx, rank)[:TOP_K]
    dest_km = dest.reshape(t * TOP_K // SC_ROWS, SC_ROWS)
    n_blocks = -(-(t * TOP_K) // MOE_BLOCK) + N_EXPERTS
    block_lo = jnp.arange(n_blocks, dtype=jnp.int32) * MOE_BLOCK
    block_e = jnp.minimum(jnp.sum(pad_ends[None, :] <= block_lo[:, None], axis=1),
                          N_EXPERTS - 1).astype(jnp.int32)
    n_used = jnp.sum(padded, keepdims=True).astype(jnp.int32) // MOE_BLOCK
    of_block = block_e[:, None] == experts[None, :]
    rows_left = jnp.sum(jnp.where(of_block, (counts + pad_starts)[None, :], 0), axis=1) - block_lo
    valid = jnp.clip(rows_left, 0, MOE_BLOCK).astype(jnp.int32)
    xs = _dispatch(xp, dest_km, n_blocks * MOE_BLOCK)
    yb = _experts(xs, block_e, n_used, valid, layer, w_gu, b_gu, w_down, b_down)
    tp = t // COMBINE_PARTS
    out = None
    for p in range(COMBINE_PARTS):
        dest_p = dest[:, p * tp:(p + 1) * tp].reshape(tp * TOP_K // SC_ROWS, SC_ROWS)
        y4 = _gather_rows(yb, dest_p).reshape(TOP_K, tp, d // 2)
        out = _combine(xt, gates, y4, p, out, ln_g, ln_b)
    return out.reshape(bsz, seq, d)


def kernel(x, mix_w_in, sc_conv_w, ssm_conv_w, ssm_conv_b, ssm_dt_bias, ssm_a_log, ssm_d, ssm_norm_w, mix_w_out, conf_w_pw1, conf_b_pw1, conf_w_dw, conf_b_dw, conf_ln_g, conf_ln_b, conf_w_pw2, conf_b_pw2, router_w, router_b, exp_w_gu, exp_b_gu, exp_w_down, exp_b_down, ln_mix_g, ln_mix_b, ln_ffn_g, ln_ffn_b):
    h = x
    for i in range(DEPTH):
        j = i // 2
        if i % 2 == 0:
            h = _even_mixer(h, mix_w_in[j], sc_conv_w[j], ssm_conv_w[j], ssm_conv_b[j],
                            ssm_dt_bias[j], ssm_a_log[j], ssm_d[j], ssm_norm_w[j], mix_w_out[j],
                            ln_mix_g[i], ln_mix_b[i])
        else:
            h = _odd_mixer(h, conf_w_pw1[j], conf_b_pw1[j], conf_w_dw[j], conf_b_dw[j],
                           conf_ln_g[j], conf_ln_b[j], conf_w_pw2[j], conf_b_pw2[j],
                           ln_mix_g[i], ln_mix_b[i])
        h = _moe_layer(h, i, router_w[i], router_b[i], exp_w_gu, exp_b_gu, exp_w_down,
                       exp_b_down, ln_ffn_g[i], ln_ffn_b[i])
    return h
```

```python
import functools
import math

import jax
import jax.numpy as jnp
from jax import lax
from jax.experimental import pallas as pl
from jax.experimental.pallas import tpu as pltpu
from jax.experimental.pallas import tpu_sc as plsc

F32 = jnp.float32
BF16 = jnp.bfloat16

D_MODEL = 1024
DEPTH = 4
ALPHA = (2 * DEPTH) ** 0.25
LN_EPS = 1e-5
SC_KERNEL = 3
SSM_HEAD_DIM = 64
SSM_HEADS = 16
SSM_GROUPS = 2
SSM_STATE = 128
SSM_CONV = 4
SSM_CHUNK = 128
SSM_INNER = D_MODEL
GROUP_COLS = SSM_INNER // SSM_GROUPS
XBC_COLS = SSM_INNER + 2 * SSM_GROUPS * SSM_STATE
CONF_KERNEL = 31
N_EXPERTS = 32
TOP_K = 4
SWIGLU_LIMIT = 7.0
SWIGLU_ALPHA = 1.702
MOE_BLOCK = 512

LANES = 128
SUBLANES = 8
CARRY_ROWS = 8
CONF_CARRY_ROWS = 32
CONF_TILE_ROWS = 32
VMEM_LIMIT = 56 * 1024 * 1024

MIX_ROWS = 256
SSD_MIX_ROWS = 512
ROUTER_ROWS = 512
COMBINE_ROWS = 256
COMBINE_PARTS = 4
SC_ROWS = 64


def _layer_norm(v, g, b):
    mu = jnp.mean(v, axis=-1, keepdims=True)
    d = v - mu
    var = jnp.mean(d * d, axis=-1, keepdims=True)
    return d * lax.rsqrt(var + LN_EPS) * g + b


def _silu(v):
    return v * jax.nn.sigmoid(v)


def _softplus(v):
    return jnp.maximum(v, 0.0) + jnp.log1p(jnp.exp(-jnp.abs(v)))


def _dot(a, b):
    return jnp.dot(a, b, preferred_element_type=F32)


def _dot_exact01(m01, v):
    v1 = v.astype(BF16)
    r1 = v - v1.astype(F32)
    v2 = r1.astype(BF16)
    v3 = (r1 - v2.astype(F32)).astype(BF16)
    return _dot(m01, v1) + _dot(m01, v2) + _dot(m01, v3)


def _even_mixer_kernel(x_ref, w_sc_ref, w_z_ref, w_xbc_ref, w_dt_ref, scw_ref, ssmw_ref,
                       ssmb_ref, dtb_ref, a_ref, dexp_ref, normw_ref, wout_ref,
                       lng_ref, lnb_ref, out_ref,
                       u_buf, xbc_buf, act_buf, dtn_buf, y_buf, state_ref):
    rows = x_ref.shape[1]
    c = pl.program_id(1)

    @pl.when(c == 0)
    def _():
        u_buf[0:CARRY_ROWS, :] = jnp.zeros((CARRY_ROWS, D_MODEL), F32)
        xbc_buf[0:CARRY_ROWS, :] = jnp.zeros((CARRY_ROWS, XBC_COLS), F32)
        state_ref[...] = jnp.zeros(state_ref.shape, F32)

    x = x_ref[0]
    xb = x.astype(BF16)

    p = _dot(xb, w_sc_ref[...])
    u = p[:, D_MODEL:2 * D_MODEL] * p[:, :D_MODEL]
    u_buf[CARRY_ROWS:CARRY_ROWS + rows, :] = u
    conv = scw_ref[2:3, :] * u
    for k in range(SC_KERNEL - 1):
        shift = SC_KERNEL - 1 - k
        conv = conv + scw_ref[k:k + 1, :] * u_buf[CARRY_ROWS - shift:CARRY_ROWS - shift + rows, :]
    y_sc = p[:, 2 * D_MODEL:] * conv
    u_buf[0:CARRY_ROWS, :] = u[rows - CARRY_ROWS:, :]

    xbc = _dot(xb, w_xbc_ref[...])
    xbc_buf[CARRY_ROWS:CARRY_ROWS + rows, :] = xbc
    cv = ssmw_ref[SSM_CONV - 1:SSM_CONV, :] * xbc + ssmb_ref[...]
    for k in range(SSM_CONV - 1):
        shift = SSM_CONV - 1 - k
        cv = cv + ssmw_ref[k:k + 1, :] * xbc_buf[CARRY_ROWS - shift:CARRY_ROWS - shift + rows, :]
    act_buf[...] = _silu(cv)
    xbc_buf[0:CARRY_ROWS, :] = xbc[rows - CARRY_ROWS:, :]

    dtn_buf[...] = _softplus(_dot(xb, w_dt_ref[...]) + dtb_ref[...])

    li = lax.broadcasted_iota(jnp.int32, (SSM_CHUNK, SSM_CHUNK), 0)
    si = lax.broadcasted_iota(jnp.int32, (SSM_CHUNK, SSM_CHUNK), 1)
    causal = li >= si
    tril = jnp.where(causal, 1.0, 0.0).astype(BF16)
    low_half = si < SSM_HEAD_DIM
    low_row = low_half[0:1, :]

    for j in range(rows // SSM_CHUNK):
        r0 = j * SSM_CHUNK
        dt_c = dtn_buf[r0:r0 + SSM_CHUNK, :]
        acum = _dot_exact01(tril, dt_c * a_ref[...])
        alast = acum[SSM_CHUNK - 1:SSM_CHUNK, :]
        acum_t = acum.T
        dt_t = dt_c.T
        e_acum = jnp.exp(acum)
        w_t = (jnp.exp(alast - acum) * dt_c).T
        chunk_decay = jnp.exp(alast)
        xs_b = act_buf[r0:r0 + SSM_CHUNK, 0:SSM_INNER].astype(BF16)
        for g in range(SSM_GROUPS):
            b_g = act_buf[r0:r0 + SSM_CHUNK, SSM_INNER + g * SSM_STATE:SSM_INNER + (g + 1) * SSM_STATE]
            c_lo = SSM_INNER + (SSM_GROUPS + g) * SSM_STATE
            c_g = act_buf[r0:r0 + SSM_CHUNK, c_lo:c_lo + SSM_STATE]
            cb = lax.dot_general(c_g.astype(BF16), b_g.astype(BF16), (((1,), (1,)), ((), ())),
                                 preferred_element_type=F32)
            b_gt = b_g.T
            for pair in range(GROUP_COLS // LANES):
                col0 = g * GROUP_COLS + pair * LANES
                h0 = col0 // SSM_HEAD_DIM
                xs_p = xs_b[:, col0:col0 + LANES]
                zero_b = jnp.zeros_like(xs_p)
                xs_lo = jnp.where(low_half, xs_p, zero_b)
                xs_hi = jnp.where(low_half, zero_b, xs_p)
                st = state_ref[g, :, pair * LANES:(pair + 1) * LANES]
                st_b = st.astype(BF16)
                lhs, lhs_s = [], []
                for h in (h0, h0 + 1):
                    seg = acum[:, h:h + 1] - acum_t[h:h + 1, :]
                    decay = jnp.exp(jnp.where(causal, seg, -jnp.inf))
                    lhs.append((cb * decay * dt_t[h:h + 1, :]).astype(BF16))
                    lhs_s.append((b_gt * w_t[h:h + 1, :]).astype(BF16))
                for h in (h0, h0 + 1):
                    lhs.append((c_g * e_acum[:, h:h + 1]).astype(BF16))
                rhs = jnp.concatenate([xs_lo, xs_hi, jnp.where(low_half, st_b, zero_b),
                                       jnp.where(low_half, zero_b, st_b)], axis=0)
                y_buf[r0:r0 + SSM_CHUNK, col0:col0 + LANES] = _dot(
                    jnp.concatenate(lhs, axis=1), rhs)
                cd_p = jnp.where(low_row, chunk_decay[:, h0:h0 + 1], chunk_decay[:, h0 + 1:h0 + 2])
                state_ref[g, :, pair * LANES:(pair + 1) * LANES] = st * cd_p + _dot(
                    jnp.concatenate(lhs_s, axis=1), jnp.concatenate([xs_lo, xs_hi], axis=0))

    z = _dot(xb, w_z_ref[...])
    y = (y_buf[...] + act_buf[:, 0:SSM_INNER] * dexp_ref[...]) * _silu(z)
    parts = []
    for g in range(SSM_GROUPS):
        yg = y[:, g * GROUP_COLS:(g + 1) * GROUP_COLS]
        ms = jnp.mean(yg * yg, axis=-1, keepdims=True)
        parts.append(yg * lax.rsqrt(ms + LN_EPS) * normw_ref[:, g * GROUP_COLS:(g + 1) * GROUP_COLS])
    mix = _dot(y_sc.astype(BF16), wout_ref[0:D_MODEL, :])
    for g in range(SSM_GROUPS):
        lo = D_MODEL + g * GROUP_COLS
        mix = mix + _dot(parts[g].astype(BF16), wout_ref[lo:lo + GROUP_COLS, :])
    out_ref[0] = _layer_norm(ALPHA * x + mix, lng_ref[...], lnb_ref[...])


def _const_spec(shape):
    nd = len(shape)
    return pl.BlockSpec(shape, lambda *_: (0,) * nd, pipeline_mode=pl.Buffered(1))


def _even_mixer(h, w_in, sc_conv_w, ssm_conv_w, ssm_conv_b, dt_bias, a_log, d_skip, norm_w,
                w_out, ln_g, ln_b):
    bsz, seq, _ = h.shape
    rows = min(SSD_MIX_ROWS, seq)
    cut_z = 3 * D_MODEL
    cut_xbc = cut_z + SSM_INNER
    cut_dt = cut_xbc + XBC_COLS
    w_sc = w_in[:, :cut_z].astype(BF16)
    w_z = w_in[:, cut_z:cut_xbc].astype(BF16)
    w_xbc = w_in[:, cut_xbc:cut_dt].astype(BF16)
    pad = LANES - SSM_HEADS
    w_dt = jnp.pad(w_in[:, cut_dt:], ((0, 0), (0, pad))).astype(BF16)
    dtb = jnp.pad(dt_bias, (0, pad)).reshape(1, LANES)
    a = -jnp.exp(a_log)
    a_n = jnp.pad(a, (0, pad)).reshape(1, LANES)
    d_exp = jnp.repeat(d_skip, SSM_HEAD_DIM).reshape(1, SSM_INNER)
    consts = [w_sc, w_z, w_xbc, w_dt, sc_conv_w, ssm_conv_w, ssm_conv_b.reshape(1, -1), dtb, a_n,
              d_exp, norm_w.reshape(1, -1), w_out.astype(BF16), ln_g.reshape(1, -1),
              ln_b.reshape(1, -1)]
    x_spec = pl.BlockSpec((1, rows, D_MODEL), lambda b, c: (b, c, 0))
    return pl.pallas_call(
        _even_mixer_kernel,
        grid=(bsz, seq // rows),
        in_specs=[x_spec] + [_const_spec(v.shape) for v in consts],
        out_specs=x_spec,
        out_shape=jax.ShapeDtypeStruct(h.shape, F32),
        scratch_shapes=[
            pltpu.VMEM((CARRY_ROWS + rows, D_MODEL), F32),
            pltpu.VMEM((CARRY_ROWS + rows, XBC_COLS), F32),
            pltpu.VMEM((rows, XBC_COLS), F32),
            pltpu.VMEM((rows, LANES), F32),
            pltpu.VMEM((rows, SSM_INNER), F32),
            pltpu.VMEM((SSM_GROUPS, SSM_STATE, GROUP_COLS), F32),
        ],
        compiler_params=pltpu.CompilerParams(
            dimension_semantics=("arbitrary", "arbitrary"), vmem_limit_bytes=VMEM_LIMIT),
        name="even_mixer",
    )(h, *consts)


def _odd_mixer_kernel(x_ref, w1_ref, b1_ref, wdw_ref, bdw_ref, g_ref, b_ref, w2_ref, b2_ref,
                      lng_ref, lnb_ref, out_ref, sh_buf, conv_buf):
    rows = x_ref.shape[1]
    c = pl.program_id(1)

    @pl.when(c == 0)
    def _():
        sh_buf[0, 0:CONF_CARRY_ROWS, :] = jnp.zeros((CONF_CARRY_ROWS, D_MODEL), F32)

    x = x_ref[0]
    u = _dot(x.astype(BF16), w1_ref[...]) + b1_ref[...]
    u = u[:, :D_MODEL] * jax.nn.sigmoid(u[:, D_MODEL:])
    sh_buf[0, CONF_CARRY_ROWS:CONF_CARRY_ROWS + rows, :] = u
    span = rows + CONF_CARRY_ROWS - SUBLANES
    for r in range(1, SUBLANES):
        sh_buf[r, 0:span, :] = sh_buf[0, r:r + span, :]

    def conv_tile(i, carry):
        base = pl.multiple_of(i * CONF_TILE_ROWS, CONF_TILE_ROWS)
        groups = CONF_TILE_ROWS // SUBLANES
        acc = [bdw_ref[...] for _ in range(groups)]
        for k in range(CONF_KERNEL):
            lo = CONF_CARRY_ROWS - (CONF_KERNEL - 1 - k)
            w_k = wdw_ref[k]
            for g in range(groups):
                row0 = base + (lo // SUBLANES + g) * SUBLANES
                acc[g] = acc[g] + w_k * sh_buf[lo % SUBLANES, pl.ds(row0, SUBLANES), :]
        for g in range(groups):
            conv_buf[pl.ds(base + g * SUBLANES, SUBLANES), :] = acc[g]
        return carry

    lax.fori_loop(0, rows // CONF_TILE_ROWS, conv_tile, 0)
    sh_buf[0, 0:CONF_CARRY_ROWS, :] = u[rows - CONF_CARRY_ROWS:, :]
    v = _silu(_layer_norm(conv_buf[...], g_ref[...], b_ref[...]))
    mix = _dot(v.astype(BF16), w2_ref[...]) + b2_ref[...]
    out_ref[0] = _layer_norm(ALPHA * x + mix, lng_ref[...], lnb_ref[...])


def _odd_mixer(h, w_pw1, b_pw1, w_dw, b_dw, ln_g_c, ln_b_c, w_pw2, b_pw2, ln_g, ln_b):
    bsz, seq, _ = h.shape
    rows = min(MIX_ROWS, seq)
    w_dw_rows = jnp.broadcast_to(w_dw[:, None, :], (CONF_KERNEL, SUBLANES, D_MODEL))
    b_dw_rows = jnp.broadcast_to(b_dw[None, :], (SUBLANES, D_MODEL))
    consts = [w_pw1.astype(BF16), b_pw1.reshape(1, -1), w_dw_rows, b_dw_rows,
              ln_g_c.reshape(1, -1), ln_b_c.reshape(1, -1), w_pw2.astype(BF16),
              b_pw2.reshape(1, -1), ln_g.reshape(1, -1), ln_b.reshape(1, -1)]
    x_spec = pl.BlockSpec((1, rows, D_MODEL), lambda b, c: (b, c, 0))
    return pl.pallas_call(
        _odd_mixer_kernel,
        grid=(bsz, seq // rows),
        in_specs=[x_spec] + [_const_spec(v.shape) for v in consts],
        out_specs=x_spec,
        out_shape=jax.ShapeDtypeStruct(h.shape, F32),
        scratch_shapes=[pltpu.VMEM((SUBLANES, CONF_CARRY_ROWS + rows, D_MODEL), F32),
                        pltpu.VMEM((rows, D_MODEL), F32)],
        compiler_params=pltpu.CompilerParams(
            dimension_semantics=("arbitrary", "arbitrary"), vmem_limit_bytes=VMEM_LIMIT),
        name="odd_mixer",
    )(h, *consts)


def _pack_halves(v):
    n = v.shape[1] // 2
    lo = lax.bitcast_convert_type(v[:, :n].astype(BF16).astype(F32), jnp.uint32)
    hi = lax.bitcast_convert_type(v[:, n:].astype(BF16).astype(F32), jnp.uint32)
    return (lo >> 16) | (hi & jnp.uint32(0xFFFF0000))


def _unpack_halves(w):
    lo = lax.bitcast_convert_type(w << 16, F32)
    hi = lax.bitcast_convert_type(w & jnp.uint32(0xFFFF0000), F32)
    return lo, hi


def _router_kernel(x_ref, w_hi_ref, w_lo_ref, b_ref, idx_ref, gate_ref, rank_ref, cnt_ref, xp_ref,
                   carry_ref):
    rows = x_ref.shape[0]
    i = pl.program_id(0)

    @pl.when(i == 0)
    def _():
        carry_ref[...] = jnp.zeros(carry_ref.shape, F32)

    x = x_ref[...]
    x1 = x.astype(BF16)
    x2 = (x - x1.astype(F32)).astype(BF16)
    logits = (_dot(x1, w_hi_ref[...]) + (_dot(x1, w_lo_ref[...]) + _dot(x2, w_hi_ref[...]))
              + b_ref[...])
    xp_ref[...] = _pack_halves(x)
    lane = lax.broadcasted_iota(jnp.int32, (rows, LANES), 1)
    lane_f = lane.astype(F32)
    work = logits
    vals, sels, idxs = [], [], []
    for _ in range(TOP_K):
        m = jnp.max(work, axis=-1, keepdims=True)
        first = jnp.min(jnp.where(work == m, lane_f, float(LANES)), axis=-1, keepdims=True)
        sel = lane_f == first
        work = jnp.where(sel, -jnp.inf, work)
        vals.append(m)
        sels.append(sel)
        idxs.append(first)
    exps = [jnp.exp(v - vals[0]) for v in vals]
    inv = 1.0 / (exps[0] + exps[1] + exps[2] + exps[3])
    onehot = jnp.zeros((rows, LANES), F32)
    for sel in sels:
        onehot = onehot + jnp.where(sel, 1.0, 0.0)
    ri = lax.broadcasted_iota(jnp.int32, (rows, rows), 0)
    ci = lax.broadcasted_iota(jnp.int32, (rows, rows), 1)
    strict = jnp.where(ri > ci, 1.0, 0.0).astype(BF16)
    before = _dot(strict, onehot.astype(BF16)) + carry_ref[0:1, :]
    idx_o = jnp.zeros((rows, LANES), F32)
    gate_o = jnp.zeros((rows, LANES), F32)
    rank_o = jnp.zeros((rows, LANES), F32)
    for k in range(TOP_K):
        rank_k = jnp.sum(jnp.where(sels[k], before, 0.0), axis=-1, keepdims=True)
        here = lane == k
        idx_o = jnp.where(here, idxs[k], idx_o)
        gate_o = jnp.where(here, exps[k] * inv, gate_o)
        rank_o = jnp.where(here, rank_k, rank_o)
    gate_ref[...] = gate_o
    idx_ref[...] = idx_o.T[0:SUBLANES, :].astype(jnp.int32)
    rank_ref[...] = rank_o.T[0:SUBLANES, :].astype(jnp.int32)
    total = carry_ref[0:1, :] + jnp.sum(onehot, axis=0, keepdims=True)
    carry_ref[...] = jnp.broadcast_to(total, carry_ref.shape)
    cnt_ref[...] = jnp.broadcast_to(total, cnt_ref.shape).astype(jnp.int32)


def _router(xt, w_r, b_r):
    t = xt.shape[0]
    rows = min(ROUTER_ROWS, t)
    pad = LANES - N_EXPERTS
    w = jnp.pad(w_r, ((0, 0), (0, pad)))
    w_hi = w.astype(BF16)
    w_lo = (w - w_hi.astype(F32)).astype(BF16)
    b = jnp.pad(b_r, (0, pad), constant_values=-1e30).reshape(1, LANES)
    km_spec = pl.BlockSpec((SUBLANES, rows), lambda i: (0, i))
    idx, gate, rank, cnt, xp = pl.pallas_call(
        _router_kernel,
        grid=(t // rows,),
        in_specs=[pl.BlockSpec((rows, D_MODEL), lambda i: (i, 0)), _const_spec(w.shape),
                  _const_spec(w.shape), _const_spec(b.shape)],
        out_specs=[km_spec, pl.BlockSpec((rows, LANES), lambda i: (i, 0)), km_spec,
                   pl.BlockSpec((SUBLANES, LANES), lambda i: (0, 0)),
                   pl.BlockSpec((rows, D_MODEL // 2), lambda i: (i, 0))],
        out_shape=[jax.ShapeDtypeStruct((SUBLANES, t), jnp.int32),
                   jax.ShapeDtypeStruct((t, LANES), F32),
                   jax.ShapeDtypeStruct((SUBLANES, t), jnp.int32),
                   jax.ShapeDtypeStruct((SUBLANES, LANES), jnp.int32),
                   jax.ShapeDtypeStruct((t, D_MODEL // 2), jnp.uint32)],
        scratch_shapes=[pltpu.VMEM((SUBLANES, LANES), F32)],
        compiler_params=pltpu.CompilerParams(dimension_semantics=("arbitrary",),
                                             vmem_limit_bytes=VMEM_LIMIT),
        name="moe_router",
    )(xt, w_hi, w_lo, b)
    return idx, gate, rank, cnt[0, :N_EXPERTS], xp


def _dest_kernel(starts_ref, idx_ref, rank_ref, dest_ref):
    acc = rank_ref[...]
    idx = idx_ref[...]
    for e in range(N_EXPERTS):
        acc = acc + jnp.where(idx == e, starts_ref[e], 0)
    dest_ref[...] = acc


def _dest_rows(pad_starts, idx, rank):
    return pl.pallas_call(
        _dest_kernel,
        in_specs=[pl.BlockSpec(memory_space=pltpu.SMEM), pl.BlockSpec(idx.shape, lambda: (0, 0)),
                  pl.BlockSpec(idx.shape, lambda: (0, 0))],
        out_specs=pl.BlockSpec(idx.shape, lambda: (0, 0)),
        out_shape=jax.ShapeDtypeStruct(idx.shape, jnp.int32),
        name="moe_dest",
    )(pad_starts, idx, rank)


def _sc_workers():
    mesh = plsc.VectorSubcoreMesh(core_axis_name="c", subcore_axis_name="s")
    return mesh, mesh.num_cores, mesh.num_cores * mesh.num_subcores


def _dispatch(xt, dest_km, n_rows):
    t = xt.shape[0]
    mesh, n_cores, n_workers = _sc_workers()
    n_ch = t // n_workers // SC_ROWS
    assert n_ch * SC_ROWS * n_workers == t and n_ch % 2 == 0

    def body(x_hbm, dest_hbm, xs_hbm, idx_v, rows_v, load_sem, scat_sem):
        wid = lax.axis_index("s") * n_cores + lax.axis_index("c")
        base = wid * (n_ch * SC_ROWS)
        for k in range(TOP_K):
            pltpu.sync_copy(dest_hbm.at[pl.ds(k * (t // SC_ROWS) + wid * n_ch, n_ch)], idx_v.at[k])

        def load(j, b):
            off = pl.multiple_of(base + j * SC_ROWS, SC_ROWS)
            return pltpu.make_async_copy(x_hbm.at[pl.ds(off, SC_ROWS)], rows_v.at[b], load_sem.at[b])

        def scatter(j, b, k):
            return pltpu.make_async_copy(rows_v.at[b], xs_hbm.at[idx_v.at[k, j]], scat_sem.at[b])

        load(0, 0).start()

        @pl.loop(0, n_ch, step=2)
        def _(j0):
            for b in range(2):
                j = j0 + b
                load(j, b).wait()

                @pl.when(j >= 1)
                def _():
                    for k in range(TOP_K):
                        scatter(j - 1, 1 - b, k).wait()

                @pl.when(j + 1 < n_ch)
                def _():
                    load(j + 1, 1 - b).start()

                for k in range(TOP_K):
                    scatter(j, b, k).start()

        for k in range(TOP_K):
            scatter(n_ch - 1, 1, k).wait()

    return pl.kernel(
        body, mesh=mesh,
        out_type=jax.ShapeDtypeStruct((n_rows, xt.shape[1]), xt.dtype),
        scratch_types=[pltpu.VMEM((TOP_K, n_ch, SC_ROWS), jnp.int32),
                       pltpu.VMEM((2, SC_ROWS, xt.shape[1]), xt.dtype),
                       pltpu.SemaphoreType.DMA((2,)), pltpu.SemaphoreType.DMA((2,))],
    )(xt, dest_km)


def _gather_rows(yb, dest_km):
    n_idx = dest_km.shape[0] * SC_ROWS
    mesh, n_cores, n_workers = _sc_workers()
    n_ch = n_idx // n_workers // SC_ROWS
    assert n_ch * SC_ROWS * n_workers == n_idx and n_ch % 2 == 0

    def body(yb_hbm, dest_hbm, out_hbm, idx_v, rows_v, gat_sem, store_sem):
        wid = lax.axis_index("s") * n_cores + lax.axis_index("c")
        base = wid * (n_ch * SC_ROWS)
        pltpu.sync_copy(dest_hbm.at[pl.ds(wid * n_ch, n_ch)], idx_v)

        def gather(j, b):
            return pltpu.make_async_copy(yb_hbm.at[idx_v.at[j]], rows_v.at[b], gat_sem.at[b])

        def store(j, b):
            off = pl.multiple_of(base + j * SC_ROWS, SC_ROWS)
            return pltpu.make_async_copy(rows_v.at[b], out_hbm.at[pl.ds(off, SC_ROWS)], store_sem.at[b])

        gather(0, 0).start()

        @pl.loop(0, n_ch, step=2)
        def _(j0):
            for b in range(2):
                j = j0 + b
                gather(j, b).wait()

                @pl.when(j >= 1)
                def _():
                    store(j - 1, 1 - b).wait()

                @pl.when(j + 1 < n_ch)
                def _():
                    gather(j + 1, 1 - b).start()

                store(j, b).start()

        store(n_ch - 1, 1).wait()

    return pl.kernel(
        body, mesh=mesh,
        out_type=jax.ShapeDtypeStruct((n_idx, yb.shape[1]), yb.dtype),
        scratch_types=[pltpu.VMEM((n_ch, SC_ROWS), jnp.int32),
                       pltpu.VMEM((2, SC_ROWS, yb.shape[1]), yb.dtype),
                       pltpu.SemaphoreType.DMA((2,)), pltpu.SemaphoreType.DMA((2,))],
    )(yb, dest_km)


def _expert_kernel(layer, be_ref, nused_ref, valid_ref, fresh_ref, slot_ref, next_ref,
                   x_ref, wgu_hbm, bgu_ref, wd_hbm, bd_ref, out_ref, wgu_buf, wd_buf, sem):
    i = pl.program_id(0)
    used = i < nused_ref[0]

    def weight_copies(expert, slot):
        return (pltpu.make_async_copy(wgu_hbm.at[layer, expert], wgu_buf.at[slot], sem.at[0, slot]),
                pltpu.make_async_copy(wd_hbm.at[layer, expert], wd_buf.at[slot], sem.at[1, slot]))

    @pl.when(i == 0)
    def _():
        for cp in weight_copies(be_ref[0], 0):
            cp.start()

    @pl.when(jnp.logical_and(used, fresh_ref[i] == 1))
    def _():
        for cp in weight_copies(be_ref[i], slot_ref[i]):
            cp.wait()

        @pl.when(next_ref[i] >= 0)
        def _():
            for cp in weight_copies(next_ref[i], 1 - slot_ref[i]):
                cp.start()

    @pl.when(used)
    def _():
        slot = slot_ref[i]
        row = lax.broadcasted_iota(jnp.int32, x_ref.shape, 0)
        x_lo, x_hi = _unpack_halves(jnp.where(row < valid_ref[i], x_ref[...], jnp.uint32(0)))
        half = x_ref.shape[1]
        gu = (_dot(x_lo.astype(BF16), wgu_buf[slot, 0:half, :].astype(BF16))
              + _dot(x_hi.astype(BF16), wgu_buf[slot, half:, :].astype(BF16)) + bgu_ref[0, 0])
        d_exp = wd_buf.shape[1]
        gate = jnp.minimum(gu[:, :d_exp], SWIGLU_LIMIT)
        up = jnp.clip(gu[:, d_exp:], -SWIGLU_LIMIT, SWIGLU_LIMIT)
        act = (up + 1.0) * (gate * jax.nn.sigmoid(SWIGLU_ALPHA * gate))
        out_ref[...] = _pack_halves(_dot(act.astype(BF16), wd_buf[slot].astype(BF16)) + bd_ref[0, 0])

    @pl.when(jnp.logical_not(used))
    def _():
        out_ref[...] = jnp.zeros(out_ref.shape, jnp.uint32)


def _experts(xs, block_e, n_used, valid, layer, w_gu, b_gu, w_down, b_down):
    n_rows = xs.shape[0]
    n_blocks = n_rows // MOE_BLOCK
    depth, n_exp, d_exp = w_down.shape[:3]
    blocks = jnp.arange(n_blocks, dtype=jnp.int32)
    fresh = jnp.concatenate([jnp.ones((1,), jnp.int32),
                             (block_e[1:] != block_e[:-1]).astype(jnp.int32)])
    ordinal = jnp.sum(jnp.where(blocks[None, :] <= blocks[:, None], fresh[None, :], 0), axis=1) - 1
    slot = (ordinal % 2).astype(jnp.int32)
    follow = jnp.sum((block_e[None, :] <= block_e[:, None]).astype(jnp.int32), axis=1)
    follow_e = jnp.sum(jnp.where(blocks[None, :] == follow[:, None], block_e[None, :], 0), axis=1)
    next_e = jnp.where(follow < n_used[0], follow_e, -1).astype(jnp.int32)

    def of_expert(i, be, *_):
        return (layer, be[i], 0, 0)

    grid_spec = pltpu.PrefetchScalarGridSpec(
        num_scalar_prefetch=6,
        grid=(n_blocks,),
        in_specs=[
            pl.BlockSpec((MOE_BLOCK, D_MODEL // 2), lambda i, be, nu, *_: (jnp.minimum(i, nu[0] - 1), 0)),
            pl.BlockSpec(memory_space=pl.ANY),
            pl.BlockSpec((1, 1, 1, 2 * d_exp), of_expert),
            pl.BlockSpec(memory_space=pl.ANY),
            pl.BlockSpec((1, 1, 1, D_MODEL), of_expert),
        ],
        out_specs=pl.BlockSpec((MOE_BLOCK, D_MODEL // 2), lambda i, *_: (i, 0)),
        scratch_shapes=[pltpu.VMEM((2, D_MODEL, 2 * d_exp), F32), pltpu.VMEM((2, d_exp, D_MODEL), F32),
                        pltpu.SemaphoreType.DMA((2, 2))],
    )
    return pl.pallas_call(
        functools.partial(_expert_kernel, layer),
        grid_spec=grid_spec,
        out_shape=jax.ShapeDtypeStruct((n_rows, D_MODEL // 2), jnp.uint32),
        compiler_params=pltpu.CompilerParams(dimension_semantics=("arbitrary",),
                                             vmem_limit_bytes=VMEM_LIMIT),
        name="moe_experts",
    )(block_e, n_used, valid, fresh, slot, next_e, xs, w_gu, b_gu.reshape(depth, n_exp, 1, -1),
      w_down, b_down.reshape(depth, n_exp, 1, -1))


def _combine_kernel(h_ref, gate_ref, y4_ref, lng_ref, lnb_ref, *rest):
    out_ref = rest[-1]
    gates = gate_ref[...]
    y_lo, y_hi = None, None
    for k in range(TOP_K):
        lo, hi = _unpack_halves(y4_ref[k])
        g = gates[:, k:k + 1]
        y_lo = g * lo if y_lo is None else y_lo + g * lo
        y_hi = g * hi if y_hi is None else y_hi + g * hi
    y = jnp.concatenate([y_lo, y_hi], axis=1)
    out_ref[...] = _layer_norm(ALPHA * h_ref[...] + y, lng_ref[...], lnb_ref[...])


def _combine(xt, gates, y4, part, earlier, ln_g, ln_b):
    t = xt.shape[0]
    tp = y4.shape[1]
    rows = min(COMBINE_ROWS, tp)
    first = part * (tp // rows)
    tok_spec = pl.BlockSpec((rows, D_MODEL), lambda i: (first + i, 0))
    in_specs = [tok_spec,
                pl.BlockSpec((rows, LANES), lambda i: (first + i, 0)),
                pl.BlockSpec((TOP_K, rows, D_MODEL // 2), lambda i: (0, i, 0)),
                _const_spec((1, D_MODEL)), _const_spec((1, D_MODEL))]
    args = [xt, gates, y4, ln_g.reshape(1, -1), ln_b.reshape(1, -1)]
    aliases = {}
    if earlier is not None:
        in_specs.append(pl.BlockSpec(memory_space=pl.ANY))
        args.append(earlier)
        aliases = {len(args) - 1: 0}
    return pl.pallas_call(
        _combine_kernel,
        grid=(tp // rows,),
        in_specs=in_specs,
        out_specs=tok_spec,
        out_shape=jax.ShapeDtypeStruct((t, D_MODEL), F32),
        input_output_aliases=aliases,
        compiler_params=pltpu.CompilerParams(dimension_semantics=("arbitrary",),
                                             vmem_limit_bytes=VMEM_LIMIT),
        name="moe_combine",
    )(*args)


def _moe_layer(h, layer, w_r, b_r, w_gu, b_gu, w_down, b_down, ln_g, ln_b):
    bsz, seq, d = h.shape
    t = bsz * seq
    xt = h.reshape(t, d)
    idx, gates, rank, counts, xp = _router(xt, w_r, b_r)
    experts = jnp.arange(N_EXPERTS, dtype=jnp.int32)
    padded = ((counts + MOE_BLOCK - 1) // MOE_BLOCK) * MOE_BLOCK
    pad_ends = jnp.sum(jnp.where(experts[None, :] <= experts[:, None], padded[None, :], 0), axis=1)
    pad_starts = pad_ends - padded
    dest = _dest_rows(pad_starts.astype(jnp.int32), idx, rank)[:TOP_K]
    dest_km = dest.reshape(t * TOP_K // SC_ROWS, SC_ROWS)
    n_blocks = -(-(t * TOP_K) // MOE_BLOCK) + N_EXPERTS
    block_lo = jnp.arange(n_blocks, dtype=jnp.int32) * MOE_BLOCK
    block_e = jnp.minimum(jnp.sum(pad_ends[None, :] <= block_lo[:, None], axis=1),
                          N_EXPERTS - 1).astype(jnp.int32)
    n_used = jnp.sum(padded, keepdims=True).astype(jnp.int32) // MOE_BLOCK
    of_block = block_e[:, None] == experts[None, :]
    rows_left = jnp.sum(jnp.where(of_block, (counts + pad_starts)[None, :], 0), axis=1) - block_lo
    valid = jnp.clip(rows_left, 0, MOE_BLOCK).astype(jnp.int32)
    xs = _dispatch(xp, dest_km, n_blocks * MOE_BLOCK)
    yb = _experts(xs, block_e, n_used, valid, layer, w_gu, b_gu, w_down, b_down)
    tp = t // COMBINE_PARTS
    out = None
    for p in range(COMBINE_PARTS):
        dest_p = dest[:, p * tp:(p + 1) * tp].reshape(tp * TOP_K // SC_ROWS, SC_ROWS)
        y4 = _gather_rows(yb, dest_p).reshape(TOP_K, tp, d // 2)
        out = _combine(xt, gates, y4, p, out, ln_g, ln_b)
    return out.reshape(bsz, seq, d)


def kernel(x, mix_w_in, sc_conv_w, ssm_conv_w, ssm_conv_b, ssm_dt_bias, ssm_a_log, ssm_d, ssm_norm_w, mix_w_out, conf_w_pw1, conf_b_pw1, conf_w_dw, conf_b_dw, conf_ln_g, conf_ln_b, conf_w_pw2, conf_b_pw2, router_w, router_b, exp_w_gu, exp_b_gu, exp_w_down, exp_b_down, ln_mix_g, ln_mix_b, ln_ffn_g, ln_ffn_b):
    h = x
    for i in range(DEPTH):
        j = i // 2
        if i % 2 == 0:
            h = _even_mixer(h, mix_w_in[j], sc_conv_w[j], ssm_conv_w[j], ssm_conv_b[j],
                            ssm_dt_bias[j], ssm_a_log[j], ssm_d[j], ssm_norm_w[j], mix_w_out[j],
                            ln_mix_g[i], ln_mix_b[i])
        else:
            h = _odd_mixer(h, conf_w_pw1[j], conf_b_pw1[j], conf_w_dw[j], conf_b_dw[j],
                           conf_ln_g[j], conf_ln_b[j], conf_w_pw2[j], conf_b_pw2[j],
                           ln_mix_g[i], ln_mix_b[i])
        h = _moe_layer(h, i, router_w[i], router_b[i], exp_w_gu, exp_b_gu, exp_w_down,
                       exp_b_down, ln_ffn_g[i], ln_ffn_b[i])
    return h
```

```python
import functools
import math

import jax
import jax.numpy as jnp
from jax import lax
from jax.experimental import pallas as pl
from jax.experimental.pallas import tpu as pltpu
from jax.experimental.pallas import tpu_sc as plsc

F32 = jnp.float32
BF16 = jnp.bfloat16

D_MODEL = 1024
DEPTH = 4
ALPHA = (2 * DEPTH) ** 0.25
LN_EPS = 1e-5
SC_KERNEL = 3
SSM_HEAD_DIM = 64
SSM_HEADS = 16
SSM_GROUPS = 2
SSM_STATE = 128
SSM_CONV = 4
SSM_CHUNK = 128
SSM_INNER = D_MODEL
GROUP_COLS = SSM_INNER // SSM_GROUPS
XBC_COLS = SSM_INNER + 2 * SSM_GROUPS * SSM_STATE
CONF_KERNEL = 31
N_EXPERTS = 32
TOP_K = 4
SWIGLU_LIMIT = 7.0
SWIGLU_ALPHA = 1.702
MOE_BLOCK = 512

LANES = 128
SUBLANES = 8
CARRY_ROWS = 8
CONF_CARRY_ROWS = 32
CONF_TILE_ROWS = 32
VMEM_LIMIT = 56 * 1024 * 1024

MIX_ROWS = 256
SSD_MIX_ROWS = 512
ROUTER_ROWS = 512
COMBINE_ROWS = 256
COMBINE_PARTS = 4
SC_ROWS = 64


def _layer_norm(v, g, b):
    mu = jnp.mean(v, axis=-1, keepdims=True)
    d = v - mu
    var = jnp.mean(d * d, axis=-1, keepdims=True)
    return d * lax.rsqrt(var + LN_EPS) * g + b


def _silu(v):
    return v * jax.nn.sigmoid(v)


def _softplus(v):
    return jnp.maximum(v, 0.0) + jnp.log1p(jnp.exp(-jnp.abs(v)))


def _dot(a, b):
    return jnp.dot(a, b, preferred_element_type=F32)


def _dot_exact01(m01, v):
    v1 = v.astype(BF16)
    r1 = v - v1.astype(F32)
    v2 = r1.astype(BF16)
    v3 = (r1 - v2.astype(F32)).astype(BF16)
    return _dot(m01, v1) + _dot(m01, v2) + _dot(m01, v3)


def _even_mixer_kernel(x_ref, w_sc_ref, w_z_ref, w_xbc_ref, w_dt_ref, scw_ref, ssmw_ref,
                       ssmb_ref, dtb_ref, a_ref, dexp_ref, normw_ref, wout_ref,
                       lng_ref, lnb_ref, out_ref,
                       u_buf, xbc_buf, act_buf, dtn_buf, y_buf, state_ref):
    rows = x_ref.shape[1]
    c = pl.program_id(1)

    @pl.when(c == 0)
    def _():
        u_buf[0:CARRY_ROWS, :] = jnp.zeros((CARRY_ROWS, D_MODEL), F32)
        xbc_buf[0:CARRY_ROWS, :] = jnp.zeros((CARRY_ROWS, XBC_COLS), F32)
        state_ref[...] = jnp.zeros(state_ref.shape, F32)

    x = x_ref[0]
    xb = x.astype(BF16)

    p = _dot(xb, w_sc_ref[...])
    u = p[:, D_MODEL:2 * D_MODEL] * p[:, :D_MODEL]
    u_buf[CARRY_ROWS:CARRY_ROWS + rows, :] = u
    conv = scw_ref[2:3, :] * u
    for k in range(SC_KERNEL - 1):
        shift = SC_KERNEL - 1 - k
        conv = conv + scw_ref[k:k + 1, :] * u_buf[CARRY_ROWS - shift:CARRY_ROWS - shift + rows, :]
    y_sc = p[:, 2 * D_MODEL:] * conv
    u_buf[0:CARRY_ROWS, :] = u[rows - CARRY_ROWS:, :]

    xbc = _dot(xb, w_xbc_ref[...])
    xbc_buf[CARRY_ROWS:CARRY_ROWS + rows, :] = xbc
    cv = ssmw_ref[SSM_CONV - 1:SSM_CONV, :] * xbc + ssmb_ref[...]
    for k in range(SSM_CONV - 1):
        shift = SSM_CONV - 1 - k
        cv = cv + ssmw_ref[k:k + 1, :] * xbc_buf[CARRY_ROWS - shift:CARRY_ROWS - shift + rows, :]
    act_buf[...] = _silu(cv)
    xbc_buf[0:CARRY_ROWS, :] = xbc[rows - CARRY_ROWS:, :]

    dtn_buf[...] = _softplus(_dot(xb, w_dt_ref[...]) + dtb_ref[...])

    li = lax.broadcasted_iota(jnp.int32, (SSM_CHUNK, SSM_CHUNK), 0)
    si = lax.broadcasted_iota(jnp.int32, (SSM_CHUNK, SSM_CHUNK), 1)
    causal = li >= si
    tril = jnp.where(causal, 1.0, 0.0).astype(BF16)
    low_half = si < SSM_HEAD_DIM
    low_row = low_half[0:1, :]

    for j in range(rows // SSM_CHUNK):
        r0 = j * SSM_CHUNK
        dt_c = dtn_buf[r0:r0 + SSM_CHUNK, :]
        acum = _dot_exact01(tril, dt_c * a_ref[...])
        alast = acum[SSM_CHUNK - 1:SSM_CHUNK, :]
        acum_t = acum.T
        dt_t = dt_c.T
        e_acum = jnp.exp(acum)
        w_t = (jnp.exp(alast - acum) * dt_c).T
        chunk_decay = jnp.exp(alast)
        xs_b = act_buf[r0:r0 + SSM_CHUNK, 0:SSM_INNER].astype(BF16)
        for g in range(SSM_GROUPS):
            b_g = act_buf[r0:r0 + SSM_CHUNK, SSM_INNER + g * SSM_STATE:SSM_INNER + (g + 1) * SSM_STATE]
            c_lo = SSM_INNER + (SSM_GROUPS + g) * SSM_STATE
            c_g = act_buf[r0:r0 + SSM_CHUNK, c_lo:c_lo + SSM_STATE]
            cb = lax.dot_general(c_g.astype(BF16), b_g.astype(BF16), (((1,), (1,)), ((), ())),
                                 preferred_element_type=F32)
            b_gt = b_g.T
            for pair in range(GROUP_COLS // LANES):
                col0 = g * GROUP_COLS + pair * LANES
                h0 = col0 // SSM_HEAD_DIM
                xs_p = xs_b[:, col0:col0 + LANES]
                zero_b = jnp.zeros_like(xs_p)
                xs_lo = jnp.where(low_half, xs_p, zero_b)
                xs_hi = jnp.where(low_half, zero_b, xs_p)
                st = state_ref[g, :, pair * LANES:(pair + 1) * LANES]
                st_b = st.astype(BF16)
                lhs, lhs_s = [], []
                for h in (h0, h0 + 1):
                    seg = acum[:, h:h + 1] - acum_t[h:h + 1, :]
                    decay = jnp.exp(jnp.where(causal, seg, -jnp.inf))
                    lhs.append((cb * decay * dt_t[h:h + 1, :]).astype(BF16))
                    lhs_s.append((b_gt * w_t[h:h + 1, :]).astype(BF16))
                for h in (h0, h0 + 1):
                    lhs.append((c_g * e_acum[:, h:h + 1]).astype(BF16))
                rhs = jnp.concatenate([xs_lo, xs_hi, jnp.where(low_half, st_b, zero_b),
                                       jnp.where(low_half, zero_b, st_b)], axis=0)
                y_buf[r0:r0 + SSM_CHUNK, col0:col0 + LANES] = _dot(
                    jnp.concatenate(lhs, axis=1), rhs)
                cd_p = jnp.where(low_row, chunk_decay[:, h0:h0 + 1], chunk_decay[:, h0 + 1:h0 + 2])
                state_ref[g, :, pair * LANES:(pair + 1) * LANES] = st * cd_p + _dot(
                    jnp.concatenate(lhs_s, axis=1), jnp.concatenate([xs_lo, xs_hi], axis=0))

    z = _dot(xb, w_z_ref[...])
    y = (y_buf[...] + act_buf[:, 0:SSM_INNER] * dexp_ref[...]) * _silu(z)
    parts = []
    for g in range(SSM_GROUPS):
        yg = y[:, g * GROUP_COLS:(g + 1) * GROUP_COLS]
        ms = jnp.mean(yg * yg, axis=-1, keepdims=True)
        parts.append(yg * lax.rsqrt(ms + LN_EPS) * normw_ref[:, g * GROUP_COLS:(g + 1) * GROUP_COLS])
    mix = _dot(y_sc.astype(BF16), wout_ref[0:D_MODEL, :])
    for g in range(SSM_GROUPS):
        lo = D_MODEL + g * GROUP_COLS
        mix = mix + _dot(parts[g].astype(BF16), wout_ref[lo:lo + GROUP_COLS, :])
    out_ref[0] = _layer_norm(ALPHA * x + mix, lng_ref[...], lnb_ref[...])


def _const_spec(shape):
    nd = len(shape)
    return pl.BlockSpec(shape, lambda *_: (0,) * nd, pipeline_mode=pl.Buffered(1))


def _even_mixer(h, w_in, sc_conv_w, ssm_conv_w, ssm_conv_b, dt_bias, a_log, d_skip, norm_w,
                w_out, ln_g, ln_b):
    bsz, seq, _ = h.shape
    rows = min(SSD_MIX_ROWS, seq)
    cut_z = 3 * D_MODEL
    cut_xbc = cut_z + SSM_INNER
    cut_dt = cut_xbc + XBC_COLS
    w_sc = w_in[:, :cut_z].astype(BF16)
    w_z = w_in[:, cut_z:cut_xbc].astype(BF16)
    w_xbc = w_in[:, cut_xbc:cut_dt].astype(BF16)
    pad = LANES - SSM_HEADS
    w_dt = jnp.pad(w_in[:, cut_dt:], ((0, 0), (0, pad))).astype(BF16)
    dtb = jnp.pad(dt_bias, (0, pad)).reshape(1, LANES)
    a = -jnp.exp(a_log)
    a_n = jnp.pad(a, (0, pad)).reshape(1, LANES)
    d_exp = jnp.repeat(d_skip, SSM_HEAD_DIM).reshape(1, SSM_INNER)
    consts = [w_sc, w_z, w_xbc, w_dt, sc_conv_w, ssm_conv_w, ssm_conv_b.reshape(1, -1), dtb, a_n,
              d_exp, norm_w.reshape(1, -1), w_out.astype(BF16), ln_g.reshape(1, -1),
              ln_b.reshape(1, -1)]
    x_spec = pl.BlockSpec((1, rows, D_MODEL), lambda b, c: (b, c, 0))
    return pl.pallas_call(
        _even_mixer_kernel,
        grid=(bsz, seq // rows),
        in_specs=[x_spec] + [_const_spec(v.shape) for v in consts],
        out_specs=x_spec,
        out_shape=jax.ShapeDtypeStruct(h.shape, F32),
        scratch_shapes=[
            pltpu.VMEM((CARRY_ROWS + rows, D_MODEL), F32),
            pltpu.VMEM((CARRY_ROWS + rows, XBC_COLS), F32),
            pltpu.VMEM((rows, XBC_COLS), F32),
            pltpu.VMEM((rows, LANES), F32),
            pltpu.VMEM((rows, SSM_INNER), F32),
            pltpu.VMEM((SSM_GROUPS, SSM_STATE, GROUP_COLS), F32),
        ],
        compiler_params=pltpu.CompilerParams(
            dimension_semantics=("arbitrary", "arbitrary"), vmem_limit_bytes=VMEM_LIMIT),
        name="even_mixer",
    )(h, *consts)


def _odd_mixer_kernel(x_ref, w1_ref, b1_ref, wdw_ref, bdw_ref, g_ref, b_ref, w2_ref, b2_ref,
                      lng_ref, lnb_ref, out_ref, sh_buf, conv_buf):
    rows = x_ref.shape[1]
    c = pl.program_id(1)

    @pl.when(c == 0)
    def _():
        sh_buf[0, 0:CONF_CARRY_ROWS, :] = jnp.zeros((CONF_CARRY_ROWS, D_MODEL), F32)

    x = x_ref[0]
    u = _dot(x.astype(BF16), w1_ref[...]) + b1_ref[...]
    u = u[:, :D_MODEL] * jax.nn.sigmoid(u[:, D_MODEL:])
    sh_buf[0, CONF_CARRY_ROWS:CONF_CARRY_ROWS + rows, :] = u
    span = rows + CONF_CARRY_ROWS - SUBLANES
    for r in range(1, SUBLANES):
        sh_buf[r, 0:span, :] = sh_buf[0, r:r + span, :]

    def conv_tile(i, carry):
        base = pl.multiple_of(i * CONF_TILE_ROWS, CONF_TILE_ROWS)
        groups = CONF_TILE_ROWS // SUBLANES
        acc = [bdw_ref[...] for _ in range(groups)]
        for k in range(CONF_KERNEL):
            lo = CONF_CARRY_ROWS - (CONF_KERNEL - 1 - k)
            w_k = wdw_ref[k]
            for g in range(groups):
                row0 = base + (lo // SUBLANES + g) * SUBLANES
                acc[g] = acc[g] + w_k * sh_buf[lo % SUBLANES, pl.ds(row0, SUBLANES), :]
        for g in range(groups):
            conv_buf[pl.ds(base + g * SUBLANES, SUBLANES), :] = acc[g]
        return carry

    lax.fori_loop(0, rows // CONF_TILE_ROWS, conv_tile, 0)
    sh_buf[0, 0:CONF_CARRY_ROWS, :] = u[rows - CONF_CARRY_ROWS:, :]
    v = _silu(_layer_norm(conv_buf[...], g_ref[...], b_ref[...]))
    mix = _dot(v.astype(BF16), w2_ref[...]) + b2_ref[...]
    out_ref[0] = _layer_norm(ALPHA * x + mix, lng_ref[...], lnb_ref[...])


def _odd_mixer(h, w_pw1, b_pw1, w_dw, b_dw, ln_g_c, ln_b_c, w_pw2, b_pw2, ln_g, ln_b):
    bsz, seq, _ = h.shape
    rows = min(MIX_ROWS, seq)
    w_dw_rows = jnp.broadcast_to(w_dw[:, None, :], (CONF_KERNEL, SUBLANES, D_MODEL))
    b_dw_rows = jnp.broadcast_to(b_dw[None, :], (SUBLANES, D_MODEL))
    consts = [w_pw1.astype(BF16), b_pw1.reshape(1, -1), w_dw_rows, b_dw_rows,
              ln_g_c.reshape(1, -1), ln_b_c.reshape(1, -1), w_pw2.astype(BF16),
              b_pw2.reshape(1, -1), ln_g.reshape(1, -1), ln_b.reshape(1, -1)]
    x_spec = pl.BlockSpec((1, rows, D_MODEL), lambda b, c: (b, c, 0))
    return pl.pallas_call(
        _odd_mixer_kernel,
        grid=(bsz, seq // rows),
        in_specs=[x_spec] + [_const_spec(v.shape) for v in consts],
        out_specs=x_spec,
        out_shape=jax.ShapeDtypeStruct(h.shape, F32),
        scratch_shapes=[pltpu.VMEM((SUBLANES, CONF_CARRY_ROWS + rows, D_MODEL), F32),
                        pltpu.VMEM((rows, D_MODEL), F32)],
        compiler_params=pltpu.CompilerParams(
            dimension_semantics=("arbitrary", "arbitrary"), vmem_limit_bytes=VMEM_LIMIT),
        name="odd_mixer",
    )(h, *consts)


def _pack_halves(v):
    n = v.shape[1] // 2
    lo = lax.bitcast_convert_type(v[:, :n].astype(BF16).astype(F32), jnp.uint32)
    hi = lax.bitcast_convert_type(v[:, n:].astype(BF16).astype(F32), jnp.uint32)
    return (lo >> 16) | (hi & jnp.uint32(0xFFFF0000))


def _unpack_halves(w):
    lo = lax.bitcast_convert_type(w << 16, F32)
    hi = lax.bitcast_convert_type(w & jnp.uint32(0xFFFF0000), F32)
    return lo, hi


def _router_kernel(x_ref, w_hi_ref, w_lo_ref, b_ref, idx_ref, gate_ref, rank_ref, cnt_ref, xp_ref,
                   carry_ref):
    rows = x_ref.shape[0]
    i = pl.program_id(0)

    @pl.when(i == 0)
    def _():
        carry_ref[...] = jnp.zeros(carry_ref.shape, F32)

    x = x_ref[...]
    x1 = x.astype(BF16)
    x2 = (x - x1.astype(F32)).astype(BF16)
    logits = (_dot(x1, w_hi_ref[...]) + (_dot(x1, w_lo_ref[...]) + _dot(x2, w_hi_ref[...]))
              + b_ref[...])
    xp_ref[...] = _pack_halves(x)
    lane = lax.broadcasted_iota(jnp.int32, (rows, LANES), 1)
    lane_f = lane.astype(F32)
    work = logits
    vals, sels, idxs = [], [], []
    for _ in range(TOP_K):
        m = jnp.max(work, axis=-1, keepdims=True)
        first = jnp.min(jnp.where(work == m, lane_f, float(LANES)), axis=-1, keepdims=True)
        sel = lane_f == first
        work = jnp.where(sel, -jnp.inf, work)
        vals.append(m)
        sels.append(sel)
        idxs.append(first)
    exps = [jnp.exp(v - vals[0]) for v in vals]
    inv = 1.0 / (exps[0] + exps[1] + exps[2] + exps[3])
    onehot = jnp.zeros((rows, LANES), F32)
    for sel in sels:
        onehot = onehot + jnp.where(sel, 1.0, 0.0)
    ri = lax.broadcasted_iota(jnp.int32, (rows, rows), 0)
    ci = lax.broadcasted_iota(jnp.int32, (rows, rows), 1)
    strict = jnp.where(ri > ci, 1.0, 0.0).astype(BF16)
    before = _dot(strict, onehot.astype(BF16)) + carry_ref[0:1, :]
    idx_o = jnp.zeros((rows, LANES), F32)
    gate_o = jnp.zeros((rows, LANES), F32)
    rank_o = jnp.zeros((rows, LANES), F32)
    for k in range(TOP_K):
        rank_k = jnp.sum(jnp.where(sels[k], before, 0.0), axis=-1, keepdims=True)
        here = lane == k
        idx_o = jnp.where(here, idxs[k], idx_o)
        gate_o = jnp.where(here, exps[k] * inv, gate_o)
        rank_o = jnp.where(here, rank_k, rank_o)
    gate_ref[...] = gate_o
    idx_ref[...] = idx_o.T[0:SUBLANES, :].astype(jnp.int32)
    rank_ref[...] = rank_o.T[0:SUBLANES, :].astype(jnp.int32)
    total = carry_ref[0:1, :] + jnp.sum(onehot, axis=0, keepdims=True)
    carry_ref[...] = jnp.broadcast_to(total, carry_ref.shape)
    cnt_ref[...] = jnp.broadcast_to(total, cnt_ref.shape).astype(jnp.int32)


def _router(xt, w_r, b_r):
    t = xt.shape[0]
    rows = min(ROUTER_ROWS, t)
    pad = LANES - N_EXPERTS
    w = jnp.pad(w_r, ((0, 0), (0, pad)))
    w_hi = w.astype(BF16)
    w_lo = (w - w_hi.astype(F32)).astype(BF16)
    b = jnp.pad(b_r, (0, pad), constant_values=-1e30).reshape(1, LANES)
    km_spec = pl.BlockSpec((SUBLANES, rows), lambda i: (0, i))
    idx, gate, rank, cnt, xp = pl.pallas_call(
        _router_kernel,
        grid=(t // rows,),
        in_specs=[pl.BlockSpec((rows, D_MODEL), lambda i: (i, 0)), _const_spec(w.shape),
                  _const_spec(w.shape), _const_spec(b.shape)],
        out_specs=[km_spec, pl.BlockSpec((rows, LANES), lambda i: (i, 0)), km_spec,
                   pl.BlockSpec((SUBLANES, LANES), lambda i: (0, 0)),
                   pl.BlockSpec((rows, D_MODEL // 2), lambda i: (i, 0))],
        out_shape=[jax.ShapeDtypeStruct((SUBLANES, t), jnp.int32),
                   jax.ShapeDtypeStruct((t, LANES), F32),
                   jax.ShapeDtypeStruct((SUBLANES, t), jnp.int32),
                   jax.ShapeDtypeStruct((SUBLANES, LANES), jnp.int32),
                   jax.ShapeDtypeStruct((t, D_MODEL // 2), jnp.uint32)],
        scratch_shapes=[pltpu.VMEM((SUBLANES, LANES), F32)],
        compiler_params=pltpu.CompilerParams(dimension_semantics=("arbitrary",),
                                             vmem_limit_bytes=VMEM_LIMIT),
        name="moe_router",
    )(xt, w_hi, w_lo, b)
    return idx, gate, rank, cnt[0, :N_EXPERTS], xp


def _dest_kernel(starts_ref, idx_ref, rank_ref, dest_ref):
    acc = rank_ref[...]
    idx = idx_ref[...]
    for e in range(N_EXPERTS):
        acc = acc + jnp.where(idx == e, starts_ref[e], 0)
    dest_ref[...] = acc


def _dest_rows(pad_starts, idx, rank):
    return pl.pallas_call(
        _dest_kernel,
        in_specs=[pl.BlockSpec(memory_space=pltpu.SMEM), pl.BlockSpec(idx.shape, lambda: (0, 0)),
                  pl.BlockSpec(idx.shape, lambda: (0, 0))],
        out_specs=pl.BlockSpec(idx.shape, lambda: (0, 0)),
        out_shape=jax.ShapeDtypeStruct(idx.shape, jnp.int32),
        name="moe_dest",
    )(pad_starts, idx, rank)


def _sc_workers():
    mesh = plsc.VectorSubcoreMesh(core_axis_name="c", subcore_axis_name="s")
    return mesh, mesh.num_cores, mesh.num_cores * mesh.num_subcores


def _dispatch(xt, dest_km, n_rows):
    t = xt.shape[0]
    mesh, n_cores, n_workers = _sc_workers()
    n_ch = t // n_workers // SC_ROWS
    assert n_ch * SC_ROWS * n_workers == t and n_ch % 2 == 0

    def body(x_hbm, dest_hbm, xs_hbm, idx_v, rows_v, load_sem, scat_sem):
        wid = lax.axis_index("s") * n_cores + lax.axis_index("c")
        base = wid * (n_ch * SC_ROWS)
        for k in range(TOP_K):
            pltpu.sync_copy(dest_hbm.at[pl.ds(k * (t // SC_ROWS) + wid * n_ch, n_ch)], idx_v.at[k])

        def load(j, b):
            off = pl.multiple_of(base + j * SC_ROWS, SC_ROWS)
            return pltpu.make_async_copy(x_hbm.at[pl.ds(off, SC_ROWS)], rows_v.at[b], load_sem.at[b])

        def scatter(j, b, k):
            return pltpu.make_async_copy(rows_v.at[b], xs_hbm.at[idx_v.at[k, j]], scat_sem.at[b])

        load(0, 0).start()

        @pl.loop(0, n_ch, step=2)
        def _(j0):
            for b in range(2):
                j = j0 + b
                load(j, b).wait()

                @pl.when(j >= 1)
                def _():
                    for k in range(TOP_K):
                        scatter(j - 1, 1 - b, k).wait()

                @pl.when(j + 1 < n_ch)
                def _():
                    load(j + 1, 1 - b).start()

                for k in range(TOP_K):
                    scatter(j, b, k).start()

        for k in range(TOP_K):
            scatter(n_ch - 1, 1, k).wait()

    return pl.kernel(
        body, mesh=mesh,
        out_type=jax.ShapeDtypeStruct((n_rows, xt.shape[1]), xt.dtype),
        scratch_types=[pltpu.VMEM((TOP_K, n_ch, SC_ROWS), jnp.int32),
                       pltpu.VMEM((2, SC_ROWS, xt.shape[1]), xt.dtype),
                       pltpu.SemaphoreType.DMA((2,)), pltpu.SemaphoreType.DMA((2,))],
    )(xt, dest_km)


def _gather_rows(yb, dest_km):
    n_idx = dest_km.shape[0] * SC_ROWS
    mesh, n_cores, n_workers = _sc_workers()
    n_ch = n_idx // n_workers // SC_ROWS
    assert n_ch * SC_ROWS * n_workers == n_idx and n_ch % 2 == 0

    def body(yb_hbm, dest_hbm, out_hbm, idx_v, rows_v, gat_sem, store_sem):
        wid = lax.axis_index("s") * n_cores + lax.axis_index("c")
        base = wid * (n_ch * SC_ROWS)
        pltpu.sync_copy(dest_hbm.at[pl.ds(wid * n_ch, n_ch)], idx_v)

        def gather(j, b):
            return pltpu.make_async_copy(yb_hbm.at[idx_v.at[j]], rows_v.at[b], gat_sem.at[b])

        def store(j, b):
            off = pl.multiple_of(base + j * SC_ROWS, SC_ROWS)
            return pltpu.make_async_copy(rows_v.at[b], out_hbm.at[pl.ds(off, SC_ROWS)], store_sem.at[b])

        gather(0, 0).start()

        @pl.loop(0, n_ch, step=2)
        def _(j0):
            for b in range(2):
                j = j0 + b
                gather(j, b).wait()

                @pl.when(j >= 1)
                def _():
                    store(j - 1, 1 - b).wait()

                @pl.when(j + 1 < n_ch)
                def _():
                    gather(j + 1, 1 - b).start()

                store(j, b).start()

        store(n_ch - 1, 1).wait()

    return pl.kernel(
        body, mesh=mesh,
        out_type=jax.ShapeDtypeStruct((n_idx, yb.shape[1]), yb.dtype),
        scratch_types=[pltpu.VMEM((n_ch, SC_ROWS), jnp.int32),
                       pltpu.VMEM((2, SC_ROWS, yb.shape[1]), yb.dtype),
                       pltpu.SemaphoreType.DMA((2,)), pltpu.SemaphoreType.DMA((2,))],
    )(yb, dest_km)


def _expert_kernel(layer, be_ref, nused_ref, valid_ref, fresh_ref, slot_ref, next_ref,
                   x_ref, wgu_hbm, bgu_ref, wd_hbm, bd_ref, out_ref, wgu_buf, wd_buf, sem,
                   wgu_bf, wd_bf):
    i = pl.program_id(0)
    used = i < nused_ref[0]

    def weight_copies(expert, slot):
        return (pltpu.make_async_copy(wgu_hbm.at[layer, expert], wgu_buf.at[slot], sem.at[0, slot]),
                pltpu.make_async_copy(wd_hbm.at[layer, expert], wd_buf.at[slot], sem.at[1, slot]))

    @pl.when(i == 0)
    def _():
        for cp in weight_copies(be_ref[0], 0):
            cp.start()

    @pl.when(jnp.logical_and(used, fresh_ref[i] == 1))
    def _():
        for cp in weight_copies(be_ref[i], slot_ref[i]):
            cp.wait()

        @pl.when(next_ref[i] >= 0)
        def _():
            for cp in weight_copies(next_ref[i], 1 - slot_ref[i]):
                cp.start()

        wgu_bf[...] = wgu_buf[slot_ref[i]].astype(BF16)
        wd_bf[...] = wd_buf[slot_ref[i]].astype(BF16)

    @pl.when(used)
    def _():
        row = lax.broadcasted_iota(jnp.int32, x_ref.shape, 0)
        x_lo, x_hi = _unpack_halves(jnp.where(row < valid_ref[i], x_ref[...], jnp.uint32(0)))
        half = x_ref.shape[1]
        gu = (_dot(x_lo.astype(BF16), wgu_bf[0:half, :])
              + _dot(x_hi.astype(BF16), wgu_bf[half:, :]) + bgu_ref[0, 0])
        d_exp = wd_buf.shape[1]
        gate = jnp.minimum(gu[:, :d_exp], SWIGLU_LIMIT)
        up = jnp.clip(gu[:, d_exp:], -SWIGLU_LIMIT, SWIGLU_LIMIT)
        act = (up + 1.0) * (gate * jax.nn.sigmoid(SWIGLU_ALPHA * gate))
        out_ref[...] = _pack_halves(_dot(act.astype(BF16), wd_bf[...]) + bd_ref[0, 0])

    @pl.when(jnp.logical_not(used))
    def _():
        out_ref[...] = jnp.zeros(out_ref.shape, jnp.uint32)


def _experts(xs, block_e, n_used, valid, layer, w_gu, b_gu, w_down, b_down):
    n_rows = xs.shape[0]
    n_blocks = n_rows // MOE_BLOCK
    depth, n_exp, d_exp = w_down.shape[:3]
    blocks = jnp.arange(n_blocks, dtype=jnp.int32)
    fresh = jnp.concatenate([jnp.ones((1,), jnp.int32),
                             (block_e[1:] != block_e[:-1]).astype(jnp.int32)])
    ordinal = jnp.sum(jnp.where(blocks[None, :] <= blocks[:, None], fresh[None, :], 0), axis=1) - 1
    slot = (ordinal % 2).astype(jnp.int32)
    follow = jnp.sum((block_e[None, :] <= block_e[:, None]).astype(jnp.int32), axis=1)
    follow_e = jnp.sum(jnp.where(blocks[None, :] == follow[:, None], block_e[None, :], 0), axis=1)
    next_e = jnp.where(follow < n_used[0], follow_e, -1).astype(jnp.int32)

    def of_expert(i, be, *_):
        return (layer, be[i], 0, 0)

    grid_spec = pltpu.PrefetchScalarGridSpec(
        num_scalar_prefetch=6,
        grid=(n_blocks,),
        in_specs=[
            pl.BlockSpec((MOE_BLOCK, D_MODEL // 2), lambda i, be, nu, *_: (jnp.minimum(i, nu[0] - 1), 0)),
            pl.BlockSpec(memory_space=pl.ANY),
            pl.BlockSpec((1, 1, 1, 2 * d_exp), of_expert),
            pl.BlockSpec(memory_space=pl.ANY),
            pl.BlockSpec((1, 1, 1, D_MODEL), of_expert),
        ],
        out_specs=pl.BlockSpec((MOE_BLOCK, D_MODEL // 2), lambda i, *_: (i, 0)),
        scratch_shapes=[pltpu.VMEM((2, D_MODEL, 2 * d_exp), F32), pltpu.VMEM((2, d_exp, D_MODEL), F32),
                        pltpu.SemaphoreType.DMA((2, 2)),
                        pltpu.VMEM((D_MODEL, 2 * d_exp), BF16), pltpu.VMEM((d_exp, D_MODEL), BF16)],
    )
    return pl.pallas_call(
        functools.partial(_expert_kernel, layer),
        grid_spec=grid_spec,
        out_shape=jax.ShapeDtypeStruct((n_rows, D_MODEL // 2), jnp.uint32),
        compiler_params=pltpu.CompilerParams(dimension_semantics=("arbitrary",),
                                             vmem_limit_bytes=VMEM_LIMIT),
        name="moe_experts",
    )(block_e, n_used, valid, fresh, slot, next_e, xs, w_gu, b_gu.reshape(depth, n_exp, 1, -1),
      w_down, b_down.reshape(depth, n_exp, 1, -1))


def _combine_kernel(h_ref, gate_ref, y4_ref, lng_ref, lnb_ref, *rest):
    out_ref = rest[-1]
    gates = gate_ref[...]
    y_lo, y_hi = None, None
    for k in range(TOP_K):
        lo, hi = _unpack_halves(y4_ref[k])
        g = gates[:, k:k + 1]
        y_lo = g * lo if y_lo is None else y_lo + g * lo
        y_hi = g * hi if y_hi is None else y_hi + g * hi
    y = jnp.concatenate([y_lo, y_hi], axis=1)
    out_ref[...] = _layer_norm(ALPHA * h_ref[...] + y, lng_ref[...], lnb_ref[...])


def _combine(xt, gates, y4, part, earlier, ln_g, ln_b):
    t = xt.shape[0]
    tp = y4.shape[1]
    rows = min(COMBINE_ROWS, tp)
    first = part * (tp // rows)
    tok_spec = pl.BlockSpec((rows, D_MODEL), lambda i: (first + i, 0))
    in_specs = [tok_spec,
                pl.BlockSpec((rows, LANES), lambda i: (first + i, 0)),
                pl.BlockSpec((TOP_K, rows, D_MODEL // 2), lambda i: (0, i, 0)),
                _const_spec((1, D_MODEL)), _const_spec((1, D_MODEL))]
    args = [xt, gates, y4, ln_g.reshape(1, -1), ln_b.reshape(1, -1)]
    aliases = {}
    if earlier is not None:
        in_specs.append(pl.BlockSpec(memory_space=pl.ANY))
        args.append(earlier)
        aliases = {len(args) - 1: 0}
    return pl.pallas_call(
        _combine_kernel,
        grid=(tp // rows,),
        in_specs=in_specs,
        out_specs=tok_spec,
        out_shape=jax.ShapeDtypeStruct((t, D_MODEL), F32),
        input_output_aliases=aliases,
        compiler_params=pltpu.CompilerParams(dimension_semantics=("arbitrary",),
                                             vmem_limit_bytes=VMEM_LIMIT),
        name="moe_combine",
    )(*args)


def _moe_layer(h, layer, w_r, b_r, w_gu, b_gu, w_down, b_down, ln_g, ln_b):
    bsz, seq, d = h.shape
    t = bsz * seq
    xt = h.reshape(t, d)
    idx, gates, rank, counts, xp = _router(xt, w_r, b_r)
    experts = jnp.arange(N_EXPERTS, dtype=jnp.int32)
    padded = ((counts + MOE_BLOCK - 1) // MOE_BLOCK) * MOE_BLOCK
    pad_ends = jnp.sum(jnp.where(experts[None, :] <= experts[:, None], padded[None, :], 0), axis=1)
    pad_starts = pad_ends - padded
    dest = _dest_rows(pad_starts.astype(jnp.int32), idx, rank)[:TOP_K]
    dest_km = dest.reshape(t * TOP_K // SC_ROWS, SC_ROWS)
    n_blocks = -(-(t * TOP_K) // MOE_BLOCK) + N_EXPERTS
    block_lo = jnp.arange(n_blocks, dtype=jnp.int32) * MOE_BLOCK
    block_e = jnp.minimum(jnp.sum(pad_ends[None, :] <= block_lo[:, None], axis=1),
                          N_EXPERTS - 1).astype(jnp.int32)
    n_used = jnp.sum(padded, keepdims=True).astype(jnp.int32) // MOE_BLOCK
    of_block = block_e[:, None] == experts[None, :]
    rows_left = jnp.sum(jnp.where(of_block, (counts + pad_starts)[None, :], 0), axis=1) - block_lo
    valid = jnp.clip(rows_left, 0, MOE_BLOCK).astype(jnp.int32)
    xs = _dispatch(xp, dest_km, n_blocks * MOE_BLOCK)
    yb = _experts(xs, block_e, n_used, valid, layer, w_gu, b_gu, w_down, b_down)
    tp = t // COMBINE_PARTS
    out = None
    for p in range(COMBINE_PARTS):
        dest_p = dest[:, p * tp:(p + 1) * tp].reshape(tp * TOP_K // SC_ROWS, SC_ROWS)
        y4 = _gather_rows(yb, dest_p).reshape(TOP_K, tp, d // 2)
        out = _combine(xt, gates, y4, p, out, ln_g, ln_b)
    return out.reshape(bsz, seq, d)


def kernel(x, mix_w_in, sc_conv_w, ssm_conv_w, ssm_conv_b, ssm_dt_bias, ssm_a_log, ssm_d, ssm_norm_w, mix_w_out, conf_w_pw1, conf_b_pw1, conf_w_dw, conf_b_dw, conf_ln_g, conf_ln_b, conf_w_pw2, conf_b_pw2, router_w, router_b, exp_w_gu, exp_b_gu, exp_w_down, exp_b_down, ln_mix_g, ln_mix_b, ln_ffn_g, ln_ffn_b):
    h = x
    for i in range(DEPTH):
        j = i // 2
        if i % 2 == 0:
            h = _even_mixer(h, mix_w_in[j], sc_conv_w[j], ssm_conv_w[j], ssm_conv_b[j],
                            ssm_dt_bias[j], ssm_a_log[j], ssm_d[j], ssm_norm_w[j], mix_w_out[j],
                            ln_mix_g[i], ln_mix_b[i])
        else:
            h = _odd_mixer(h, conf_w_pw1[j], conf_b_pw1[j], conf_w_dw[j], conf_b_dw[j],
                           conf_ln_g[j], conf_ln_b[j], conf_w_pw2[j], conf_b_pw2[j],
                           ln_mix_g[i], ln_mix_b[i])
        h = _moe_layer(h, i, router_w[i], router_b[i], exp_w_gu, exp_b_gu, exp_w_down,
                       exp_b_down, ln_ffn_g[i], ln_ffn_b[i])
    return h
```

```python
import functools
import math

import jax
import jax.numpy as jnp
from jax import lax
from jax.experimental import pallas as pl
from jax.experimental.pallas import tpu as pltpu
from jax.experimental.pallas import tpu_sc as plsc

F32 = jnp.float32
BF16 = jnp.bfloat16

D_MODEL = 1024
DEPTH = 4
ALPHA = (2 * DEPTH) ** 0.25
LN_EPS = 1e-5
SC_KERNEL = 3
SSM_HEAD_DIM = 64
SSM_HEADS = 16
SSM_GROUPS = 2
SSM_STATE = 128
SSM_CONV = 4
SSM_CHUNK = 128
SSM_INNER = D_MODEL
GROUP_COLS = SSM_INNER // SSM_GROUPS
XBC_COLS = SSM_INNER + 2 * SSM_GROUPS * SSM_STATE
CONF_KERNEL = 31
N_EXPERTS = 32
TOP_K = 4
SWIGLU_LIMIT = 7.0
SWIGLU_ALPHA = 1.702
MOE_BLOCK = 512

LANES = 128
SUBLANES = 8
CARRY_ROWS = 8
CONF_CARRY_ROWS = 32
CONF_TILE_ROWS = 32
VMEM_LIMIT = 56 * 1024 * 1024

MIX_ROWS = 256
SSD_MIX_ROWS = 512
ROUTER_ROWS = 512
COMBINE_ROWS = 512
COMBINE_PARTS = 4
SC_ROWS = 64


def _layer_norm(v, g, b):
    mu = jnp.mean(v, axis=-1, keepdims=True)
    d = v - mu
    var = jnp.mean(d * d, axis=-1, keepdims=True)
    return d * lax.rsqrt(var + LN_EPS) * g + b


def _silu(v):
    return v * jax.nn.sigmoid(v)


def _softplus(v):
    return jnp.maximum(v, 0.0) + jnp.log1p(jnp.exp(-jnp.abs(v)))


def _dot(a, b):
    return jnp.dot(a, b, preferred_element_type=F32)


def _dot_exact01(m01, v):
    v1 = v.astype(BF16)
    r1 = v - v1.astype(F32)
    v2 = r1.astype(BF16)
    v3 = (r1 - v2.astype(F32)).astype(BF16)
    return _dot(m01, v1) + _dot(m01, v2) + _dot(m01, v3)


def _even_mixer_kernel(x_ref, w_sc_ref, w_z_ref, w_xbc_ref, w_dt_ref, scw_ref, ssmw_ref,
                       ssmb_ref, dtb_ref, a_ref, dexp_ref, normw_ref, wout_ref,
                       lng_ref, lnb_ref, out_ref,
                       u_buf, xbc_buf, act_buf, dtn_buf, y_buf, state_ref):
    rows = x_ref.shape[1]
    c = pl.program_id(1)

    @pl.when(c == 0)
    def _():
        u_buf[0:CARRY_ROWS, :] = jnp.zeros((CARRY_ROWS, D_MODEL), F32)
        xbc_buf[0:CARRY_ROWS, :] = jnp.zeros((CARRY_ROWS, XBC_COLS), F32)
        state_ref[...] = jnp.zeros(state_ref.shape, F32)

    x = x_ref[0]
    xb = x.astype(BF16)

    p = _dot(xb, w_sc_ref[...])
    u = p[:, D_MODEL:2 * D_MODEL] * p[:, :D_MODEL]
    u_buf[CARRY_ROWS:CARRY_ROWS + rows, :] = u
    conv = scw_ref[2:3, :] * u
    for k in range(SC_KERNEL - 1):
        shift = SC_KERNEL - 1 - k
        conv = conv + scw_ref[k:k + 1, :] * u_buf[CARRY_ROWS - shift:CARRY_ROWS - shift + rows, :]
    y_sc = p[:, 2 * D_MODEL:] * conv
    u_buf[0:CARRY_ROWS, :] = u[rows - CARRY_ROWS:, :]

    xbc = _dot(xb, w_xbc_ref[...])
    xbc_buf[CARRY_ROWS:CARRY_ROWS + rows, :] = xbc
    cv = ssmw_ref[SSM_CONV - 1:SSM_CONV, :] * xbc + ssmb_ref[...]
    for k in range(SSM_CONV - 1):
        shift = SSM_CONV - 1 - k
        cv = cv + ssmw_ref[k:k + 1, :] * xbc_buf[CARRY_ROWS - shift:CARRY_ROWS - shift + rows, :]
    act_buf[...] = _silu(cv)
    xbc_buf[0:CARRY_ROWS, :] = xbc[rows - CARRY_ROWS:, :]

    dtn_buf[...] = _softplus(_dot(xb, w_dt_ref[...]) + dtb_ref[...])

    li = lax.broadcasted_iota(jnp.int32, (SSM_CHUNK, SSM_CHUNK), 0)
    si = lax.broadcasted_iota(jnp.int32, (SSM_CHUNK, SSM_CHUNK), 1)
    causal = li >= si
    tril = jnp.where(causal, 1.0, 0.0).astype(BF16)
    low_half = si < SSM_HEAD_DIM
    low_row = low_half[0:1, :]

    for j in range(rows // SSM_CHUNK):
        r0 = j * SSM_CHUNK
        dt_c = dtn_buf[r0:r0 + SSM_CHUNK, :]
        acum = _dot_exact01(tril, dt_c * a_ref[...])
        alast = acum[SSM_CHUNK - 1:SSM_CHUNK, :]
        acum_t = acum.T
        dt_t = dt_c.T
        e_acum = jnp.exp(acum)
        w_t = (jnp.exp(alast - acum) * dt_c).T
        chunk_decay = jnp.exp(alast)
        xs_b = act_buf[r0:r0 + SSM_CHUNK, 0:SSM_INNER].astype(BF16)
        for g in range(SSM_GROUPS):
            b_g = act_buf[r0:r0 + SSM_CHUNK, SSM_INNER + g * SSM_STATE:SSM_INNER + (g + 1) * SSM_STATE]
            c_lo = SSM_INNER + (SSM_GROUPS + g) * SSM_STATE
            c_g = act_buf[r0:r0 + SSM_CHUNK, c_lo:c_lo + SSM_STATE]
            cb = lax.dot_general(c_g.astype(BF16), b_g.astype(BF16), (((1,), (1,)), ((), ())),
                                 preferred_element_type=F32)
            b_gt = b_g.T
            for pair in range(GROUP_COLS // LANES):
                col0 = g * GROUP_COLS + pair * LANES
                h0 = col0 // SSM_HEAD_DIM
                xs_p = xs_b[:, col0:col0 + LANES]
                zero_b = jnp.zeros_like(xs_p)
                xs_lo = jnp.where(low_half, xs_p, zero_b)
                xs_hi = jnp.where(low_half, zero_b, xs_p)
                st = state_ref[g, :, pair * LANES:(pair + 1) * LANES]
                st_b = st.astype(BF16)
                lhs, lhs_s = [], []
                for h in (h0, h0 + 1):
                    seg = acum[:, h:h + 1] - acum_t[h:h + 1, :]
                    decay = jnp.exp(jnp.where(causal, seg, -jnp.inf))
                    lhs.append((cb * decay * dt_t[h:h + 1, :]).astype(BF16))
                    lhs_s.append((b_gt * w_t[h:h + 1, :]).astype(BF16))
                for h in (h0, h0 + 1):
                    lhs.append((c_g * e_acum[:, h:h + 1]).astype(BF16))
                rhs = jnp.concatenate([xs_lo, xs_hi, jnp.where(low_half, st_b, zero_b),
                                       jnp.where(low_half, zero_b, st_b)], axis=0)
                y_buf[r0:r0 + SSM_CHUNK, col0:col0 + LANES] = _dot(
                    jnp.concatenate(lhs, axis=1), rhs)
                cd_p = jnp.where(low_row, chunk_decay[:, h0:h0 + 1], chunk_decay[:, h0 + 1:h0 + 2])
                state_ref[g, :, pair * LANES:(pair + 1) * LANES] = st * cd_p + _dot(
                    jnp.concatenate(lhs_s, axis=1), jnp.concatenate([xs_lo, xs_hi], axis=0))

    z = _dot(xb, w_z_ref[...])
    y = (y_buf[...] + act_buf[:, 0:SSM_INNER] * dexp_ref[...]) * _silu(z)
    parts = []
    for g in range(SSM_GROUPS):
        yg = y[:, g * GROUP_COLS:(g + 1) * GROUP_COLS]
        ms = jnp.mean(yg * yg, axis=-1, keepdims=True)
        parts.append(yg * lax.rsqrt(ms + LN_EPS) * normw_ref[:, g * GROUP_COLS:(g + 1) * GROUP_COLS])
    mix = _dot(y_sc.astype(BF16), wout_ref[0:D_MODEL, :])
    for g in range(SSM_GROUPS):
        lo = D_MODEL + g * GROUP_COLS
        mix = mix + _dot(parts[g].astype(BF16), wout_ref[lo:lo + GROUP_COLS, :])
    out_ref[0] = _layer_norm(ALPHA * x + mix, lng_ref[...], lnb_ref[...])


def _const_spec(shape):
    nd = len(shape)
    return pl.BlockSpec(shape, lambda *_: (0,) * nd, pipeline_mode=pl.Buffered(1))


def _even_mixer(h, w_in, sc_conv_w, ssm_conv_w, ssm_conv_b, dt_bias, a_log, d_skip, norm_w,
                w_out, ln_g, ln_b):
    bsz, seq, _ = h.shape
    rows = min(SSD_MIX_ROWS, seq)
    cut_z = 3 * D_MODEL
    cut_xbc = cut_z + SSM_INNER
    cut_dt = cut_xbc + XBC_COLS
    w_sc = w_in[:, :cut_z].astype(BF16)
    w_z = w_in[:, cut_z:cut_xbc].astype(BF16)
    w_xbc = w_in[:, cut_xbc:cut_dt].astype(BF16)
    pad = LANES - SSM_HEADS
    w_dt = jnp.pad(w_in[:, cut_dt:], ((0, 0), (0, pad))).astype(BF16)
    dtb = jnp.pad(dt_bias, (0, pad)).reshape(1, LANES)
    a = -jnp.exp(a_log)
    a_n = jnp.pad(a, (0, pad)).reshape(1, LANES)
    d_exp = jnp.repeat(d_skip, SSM_HEAD_DIM).reshape(1, SSM_INNER)
    consts = [w_sc, w_z, w_xbc, w_dt, sc_conv_w, ssm_conv_w, ssm_conv_b.reshape(1, -1), dtb, a_n,
              d_exp, norm_w.reshape(1, -1), w_out.astype(BF16), ln_g.reshape(1, -1),
              ln_b.reshape(1, -1)]
    x_spec = pl.BlockSpec((1, rows, D_MODEL), lambda b, c: (b, c, 0))
    return pl.pallas_call(
        _even_mixer_kernel,
        grid=(bsz, seq // rows),
        in_specs=[x_spec] + [_const_spec(v.shape) for v in consts],
        out_specs=x_spec,
        out_shape=jax.ShapeDtypeStruct(h.shape, F32),
        scratch_shapes=[
            pltpu.VMEM((CARRY_ROWS + rows, D_MODEL), F32),
            pltpu.VMEM((CARRY_ROWS + rows, XBC_COLS), F32),
            pltpu.VMEM((rows, XBC_COLS), F32),
            pltpu.VMEM((rows, LANES), F32),
            pltpu.VMEM((rows, SSM_INNER), F32),
            pltpu.VMEM((SSM_GROUPS, SSM_STATE, GROUP_COLS), F32),
        ],
        compiler_params=pltpu.CompilerParams(
            dimension_semantics=("arbitrary", "arbitrary"), vmem_limit_bytes=VMEM_LIMIT),
        name="even_mixer",
    )(h, *consts)


def _odd_mixer_kernel(x_ref, w1_ref, b1_ref, wdw_ref, bdw_ref, g_ref, b_ref, w2_ref, b2_ref,
                      lng_ref, lnb_ref, out_ref, sh_buf, conv_buf):
    rows = x_ref.shape[1]
    c = pl.program_id(1)

    @pl.when(c == 0)
    def _():
        sh_buf[0, 0:CONF_CARRY_ROWS, :] = jnp.zeros((CONF_CARRY_ROWS, D_MODEL), F32)

    x = x_ref[0]
    u = _dot(x.astype(BF16), w1_ref[...]) + b1_ref[...]
    u = u[:, :D_MODEL] * jax.nn.sigmoid(u[:, D_MODEL:])
    sh_buf[0, CONF_CARRY_ROWS:CONF_CARRY_ROWS + rows, :] = u
    span = rows + CONF_CARRY_ROWS - SUBLANES
    for r in range(1, SUBLANES):
        sh_buf[r, 0:span, :] = sh_buf[0, r:r + span, :]

    def conv_tile(i, carry):
        base = pl.multiple_of(i * CONF_TILE_ROWS, CONF_TILE_ROWS)
        groups = CONF_TILE_ROWS // SUBLANES
        acc = [bdw_ref[...] for _ in range(groups)]
        for k in range(CONF_KERNEL):
            lo = CONF_CARRY_ROWS - (CONF_KERNEL - 1 - k)
            w_k = wdw_ref[k]
            for g in range(groups):
                row0 = base + (lo // SUBLANES + g) * SUBLANES
                acc[g] = acc[g] + w_k * sh_buf[lo % SUBLANES, pl.ds(row0, SUBLANES), :]
        for g in range(groups):
            conv_buf[pl.ds(base + g * SUBLANES, SUBLANES), :] = acc[g]
        return carry

    lax.fori_loop(0, rows // CONF_TILE_ROWS, conv_tile, 0)
    sh_buf[0, 0:CONF_CARRY_ROWS, :] = u[rows - CONF_CARRY_ROWS:, :]
    v = _silu(_layer_norm(conv_buf[...], g_ref[...], b_ref[...]))
    mix = _dot(v.astype(BF16), w2_ref[...]) + b2_ref[...]
    out_ref[0] = _layer_norm(ALPHA * x + mix, lng_ref[...], lnb_ref[...])


def _odd_mixer(h, w_pw1, b_pw1, w_dw, b_dw, ln_g_c, ln_b_c, w_pw2, b_pw2, ln_g, ln_b):
    bsz, seq, _ = h.shape
    rows = min(MIX_ROWS, seq)
    w_dw_rows = jnp.broadcast_to(w_dw[:, None, :], (CONF_KERNEL, SUBLANES, D_MODEL))
    b_dw_rows = jnp.broadcast_to(b_dw[None, :], (SUBLANES, D_MODEL))
    consts = [w_pw1.astype(BF16), b_pw1.reshape(1, -1), w_dw_rows, b_dw_rows,
              ln_g_c.reshape(1, -1), ln_b_c.reshape(1, -1), w_pw2.astype(BF16),
              b_pw2.reshape(1, -1), ln_g.reshape(1, -1), ln_b.reshape(1, -1)]
    x_spec = pl.BlockSpec((1, rows, D_MODEL), lambda b, c: (b, c, 0))
    return pl.pallas_call(
        _odd_mixer_kernel,
        grid=(bsz, seq // rows),
        in_specs=[x_spec] + [_const_spec(v.shape) for v in consts],
        out_specs=x_spec,
        out_shape=jax.ShapeDtypeStruct(h.shape, F32),
        scratch_shapes=[pltpu.VMEM((SUBLANES, CONF_CARRY_ROWS + rows, D_MODEL), F32),
                        pltpu.VMEM((rows, D_MODEL), F32)],
        compiler_params=pltpu.CompilerParams(
            dimension_semantics=("arbitrary", "arbitrary"), vmem_limit_bytes=VMEM_LIMIT),
        name="odd_mixer",
    )(h, *consts)


def _pack_halves(v):
    n = v.shape[1] // 2
    lo = lax.bitcast_convert_type(v[:, :n].astype(BF16).astype(F32), jnp.uint32)
    hi = lax.bitcast_convert_type(v[:, n:].astype(BF16).astype(F32), jnp.uint32)
    return (lo >> 16) | (hi & jnp.uint32(0xFFFF0000))


def _unpack_halves(w):
    lo = lax.bitcast_convert_type(w << 16, F32)
    hi = lax.bitcast_convert_type(w & jnp.uint32(0xFFFF0000), F32)
    return lo, hi


def _router_kernel(x_ref, w_hi_ref, w_lo_ref, b_ref, idx_ref, gate_ref, rank_ref, cnt_ref, xp_ref,
                   carry_ref):
    rows = x_ref.shape[0]
    i = pl.program_id(0)

    @pl.when(i == 0)
    def _():
        carry_ref[...] = jnp.zeros(carry_ref.shape, F32)

    x = x_ref[...]
    x1 = x.astype(BF16)
    x2 = (x - x1.astype(F32)).astype(BF16)
    logits = (_dot(x1, w_hi_ref[...]) + (_dot(x1, w_lo_ref[...]) + _dot(x2, w_hi_ref[...]))
              + b_ref[...])
    xp_ref[...] = _pack_halves(x)
    lane = lax.broadcasted_iota(jnp.int32, (rows, LANES), 1)
    lane_f = lane.astype(F32)
    work = logits
    vals, sels, idxs = [], [], []
    for _ in range(TOP_K):
        m = jnp.max(work, axis=-1, keepdims=True)
        first = jnp.min(jnp.where(work == m, lane_f, float(LANES)), axis=-1, keepdims=True)
        sel = lane_f == first
        work = jnp.where(sel, -jnp.inf, work)
        vals.append(m)
        sels.append(sel)
        idxs.append(first)
    exps = [jnp.exp(v - vals[0]) for v in vals]
    inv = 1.0 / (exps[0] + exps[1] + exps[2] + exps[3])
    onehot = jnp.zeros((rows, LANES), F32)
    for sel in sels:
        onehot = onehot + jnp.where(sel, 1.0, 0.0)
    ri = lax.broadcasted_iota(jnp.int32, (rows, rows), 0)
    ci = lax.broadcasted_iota(jnp.int32, (rows, rows), 1)
    strict = jnp.where(ri > ci, 1.0, 0.0).astype(BF16)
    before = _dot(strict, onehot.astype(BF16)) + carry_ref[0:1, :]
    idx_o = jnp.zeros((rows, LANES), F32)
    gate_o = jnp.zeros((rows, LANES), F32)
    rank_o = jnp.zeros((rows, LANES), F32)
    for k in range(TOP_K):
        rank_k = jnp.sum(jnp.where(sels[k], before, 0.0), axis=-1, keepdims=True)
        here = lane == k
        idx_o = jnp.where(here, idxs[k], idx_o)
        gate_o = jnp.where(here, exps[k] * inv, gate_o)
        rank_o = jnp.where(here, rank_k, rank_o)
    gate_ref[...] = gate_o
    idx_ref[...] = idx_o.T[0:SUBLANES, :].astype(jnp.int32)
    rank_ref[...] = rank_o.T[0:SUBLANES, :].astype(jnp.int32)
    total = carry_ref[0:1, :] + jnp.sum(onehot, axis=0, keepdims=True)
    carry_ref[...] = jnp.broadcast_to(total, carry_ref.shape)
    cnt_ref[...] = jnp.broadcast_to(total, cnt_ref.shape).astype(jnp.int32)


def _router(xt, w_r, b_r):
    t = xt.shape[0]
    rows = min(ROUTER_ROWS, t)
    pad = LANES - N_EXPERTS
    w = jnp.pad(w_r, ((0, 0), (0, pad)))
    w_hi = w.astype(BF16)
    w_lo = (w - w_hi.astype(F32)).astype(BF16)
    b = jnp.pad(b_r, (0, pad), constant_values=-1e30).reshape(1, LANES)
    km_spec = pl.BlockSpec((SUBLANES, rows), lambda i: (0, i))
    idx, gate, rank, cnt, xp = pl.pallas_call(
        _router_kernel,
        grid=(t // rows,),
        in_specs=[pl.BlockSpec((rows, D_MODEL), lambda i: (i, 0)), _const_spec(w.shape),
                  _const_spec(w.shape), _const_spec(b.shape)],
        out_specs=[km_spec, pl.BlockSpec((rows, LANES), lambda i: (i, 0)), km_spec,
                   pl.BlockSpec((SUBLANES, LANES), lambda i: (0, 0)),
                   pl.BlockSpec((rows, D_MODEL // 2), lambda i: (i, 0))],
        out_shape=[jax.ShapeDtypeStruct((SUBLANES, t), jnp.int32),
                   jax.ShapeDtypeStruct((t, LANES), F32),
                   jax.ShapeDtypeStruct((SUBLANES, t), jnp.int32),
                   jax.ShapeDtypeStruct((SUBLANES, LANES), jnp.int32),
                   jax.ShapeDtypeStruct((t, D_MODEL // 2), jnp.uint32)],
        scratch_shapes=[pltpu.VMEM((SUBLANES, LANES), F32)],
        compiler_params=pltpu.CompilerParams(dimension_semantics=("arbitrary",),
                                             vmem_limit_bytes=VMEM_LIMIT),
        name="moe_router",
    )(xt, w_hi, w_lo, b)
    return idx, gate, rank, cnt[0, :N_EXPERTS], xp


def _dest_kernel(starts_ref, idx_ref, rank_ref, dest_ref):
    acc = rank_ref[...]
    idx = idx_ref[...]
    for e in range(N_EXPERTS):
        acc = acc + jnp.where(idx == e, starts_ref[e], 0)
    dest_ref[...] = acc


def _dest_rows(pad_starts, idx, rank):
    return pl.pallas_call(
        _dest_kernel,
        in_specs=[pl.BlockSpec(memory_space=pltpu.SMEM), pl.BlockSpec(idx.shape, lambda: (0, 0)),
                  pl.BlockSpec(idx.shape, lambda: (0, 0))],
        out_specs=pl.BlockSpec(idx.shape, lambda: (0, 0)),
        out_shape=jax.ShapeDtypeStruct(idx.shape, jnp.int32),
        name="moe_dest",
    )(pad_starts, idx, rank)


def _sc_workers():
    mesh = plsc.VectorSubcoreMesh(core_axis_name="c", subcore_axis_name="s")
    return mesh, mesh.num_cores, mesh.num_cores * mesh.num_subcores


def _dispatch(xt, dest_km, n_rows):
    t = xt.shape[0]
    mesh, n_cores, n_workers = _sc_workers()
    n_ch = t // n_workers // SC_ROWS
    assert n_ch * SC_ROWS * n_workers == t and n_ch % 2 == 0

    def body(x_hbm, dest_hbm, xs_hbm, idx_v, rows_v, load_sem, scat_sem):
        wid = lax.axis_index("s") * n_cores + lax.axis_index("c")
        base = wid * (n_ch * SC_ROWS)
        for k in range(TOP_K):
            pltpu.sync_copy(dest_hbm.at[pl.ds(k * (t // SC_ROWS) + wid * n_ch, n_ch)], idx_v.at[k])

        def load(j, b):
            off = pl.multiple_of(base + j * SC_ROWS, SC_ROWS)
            return pltpu.make_async_copy(x_hbm.at[pl.ds(off, SC_ROWS)], rows_v.at[b], load_sem.at[b])

        def scatter(j, b, k):
            return pltpu.make_async_copy(rows_v.at[b], xs_hbm.at[idx_v.at[k, j]], scat_sem.at[b])

        load(0, 0).start()

        @pl.loop(0, n_ch, step=2)
        def _(j0):
            for b in range(2):
                j = j0 + b
                load(j, b).wait()

                @pl.when(j >= 1)
                def _():
                    for k in range(TOP_K):
                        scatter(j - 1, 1 - b, k).wait()

                @pl.when(j + 1 < n_ch)
                def _():
                    load(j + 1, 1 - b).start()

                for k in range(TOP_K):
                    scatter(j, b, k).start()

        for k in range(TOP_K):
            scatter(n_ch - 1, 1, k).wait()

    return pl.kernel(
        body, mesh=mesh,
        out_type=jax.ShapeDtypeStruct((n_rows, xt.shape[1]), xt.dtype),
        scratch_types=[pltpu.VMEM((TOP_K, n_ch, SC_ROWS), jnp.int32),
                       pltpu.VMEM((2, SC_ROWS, xt.shape[1]), xt.dtype),
                       pltpu.SemaphoreType.DMA((2,)), pltpu.SemaphoreType.DMA((2,))],
    )(xt, dest_km)


def _gather_rows(yb, dest_km):
    n_idx = dest_km.shape[0] * SC_ROWS
    mesh, n_cores, n_workers = _sc_workers()
    n_ch = n_idx // n_workers // SC_ROWS
    assert n_ch * SC_ROWS * n_workers == n_idx and n_ch % 2 == 0

    def body(yb_hbm, dest_hbm, out_hbm, idx_v, rows_v, gat_sem, store_sem):
        wid = lax.axis_index("s") * n_cores + lax.axis_index("c")
        base = wid * (n_ch * SC_ROWS)
        pltpu.sync_copy(dest_hbm.at[pl.ds(wid * n_ch, n_ch)], idx_v)

        def gather(j, b):
            return pltpu.make_async_copy(yb_hbm.at[idx_v.at[j]], rows_v.at[b], gat_sem.at[b])

        def store(j, b):
            off = pl.multiple_of(base + j * SC_ROWS, SC_ROWS)
            return pltpu.make_async_copy(rows_v.at[b], out_hbm.at[pl.ds(off, SC_ROWS)], store_sem.at[b])

        gather(0, 0).start()

        @pl.loop(0, n_ch, step=2)
        def _(j0):
            for b in range(2):
                j = j0 + b
                gather(j, b).wait()

                @pl.when(j >= 1)
                def _():
                    store(j - 1, 1 - b).wait()

                @pl.when(j + 1 < n_ch)
                def _():
                    gather(j + 1, 1 - b).start()

                store(j, b).start()

        store(n_ch - 1, 1).wait()

    return pl.kernel(
        body, mesh=mesh,
        out_type=jax.ShapeDtypeStruct((n_idx, yb.shape[1]), yb.dtype),
        scratch_types=[pltpu.VMEM((n_ch, SC_ROWS), jnp.int32),
                       pltpu.VMEM((2, SC_ROWS, yb.shape[1]), yb.dtype),
                       pltpu.SemaphoreType.DMA((2,)), pltpu.SemaphoreType.DMA((2,))],
    )(yb, dest_km)


def _expert_kernel(layer, be_ref, nused_ref, valid_ref, fresh_ref, slot_ref, next_ref,
                   x_ref, wgu_hbm, bgu_ref, wd_hbm, bd_ref, out_ref, wgu_buf, wd_buf, sem):
    i = pl.program_id(0)
    used = i < nused_ref[0]

    def weight_copies(expert, slot):
        return (pltpu.make_async_copy(wgu_hbm.at[layer, expert], wgu_buf.at[slot], sem.at[0, slot]),
                pltpu.make_async_copy(wd_hbm.at[layer, expert], wd_buf.at[slot], sem.at[1, slot]))

    @pl.when(i == 0)
    def _():
        for cp in weight_copies(be_ref[0], 0):
            cp.start()

    @pl.when(jnp.logical_and(used, fresh_ref[i] == 1))
    def _():
        for cp in weight_copies(be_ref[i], slot_ref[i]):
            cp.wait()

        @pl.when(next_ref[i] >= 0)
        def _():
            for cp in weight_copies(next_ref[i], 1 - slot_ref[i]):
                cp.start()

    @pl.when(used)
    def _():
        slot = slot_ref[i]
        row = lax.broadcasted_iota(jnp.int32, x_ref.shape, 0)
        x_lo, x_hi = _unpack_halves(jnp.where(row < valid_ref[i], x_ref[...], jnp.uint32(0)))
        half = x_ref.shape[1]
        gu = (_dot(x_lo.astype(BF16), wgu_buf[slot, 0:half, :].astype(BF16))
              + _dot(x_hi.astype(BF16), wgu_buf[slot, half:, :].astype(BF16)) + bgu_ref[0, 0])
        d_exp = wd_buf.shape[1]
        gate = jnp.minimum(gu[:, :d_exp], SWIGLU_LIMIT)
        up = jnp.clip(gu[:, d_exp:], -SWIGLU_LIMIT, SWIGLU_LIMIT)
        act = (up + 1.0) * (gate * jax.nn.sigmoid(SWIGLU_ALPHA * gate))
        out_ref[...] = _pack_halves(_dot(act.astype(BF16), wd_buf[slot].astype(BF16)) + bd_ref[0, 0])

    @pl.when(jnp.logical_not(used))
    def _():
        out_ref[...] = jnp.zeros(out_ref.shape, jnp.uint32)


def _experts(xs, block_e, n_used, valid, layer, w_gu, b_gu, w_down, b_down):
    n_rows = xs.shape[0]
    n_blocks = n_rows // MOE_BLOCK
    depth, n_exp, d_exp = w_down.shape[:3]
    blocks = jnp.arange(n_blocks, dtype=jnp.int32)
    fresh = jnp.concatenate([jnp.ones((1,), jnp.int32),
                             (block_e[1:] != block_e[:-1]).astype(jnp.int32)])
    ordinal = jnp.sum(jnp.where(blocks[None, :] <= blocks[:, None], fresh[None, :], 0), axis=1) - 1
    slot = (ordinal % 2).astype(jnp.int32)
    follow = jnp.sum((block_e[None, :] <= block_e[:, None]).astype(jnp.int32), axis=1)
    follow_e = jnp.sum(jnp.where(blocks[None, :] == follow[:, None], block_e[None, :], 0), axis=1)
    next_e = jnp.where(follow < n_used[0], follow_e, -1).astype(jnp.int32)

    def of_expert(i, be, *_):
        return (layer, be[i], 0, 0)

    grid_spec = pltpu.PrefetchScalarGridSpec(
        num_scalar_prefetch=6,
        grid=(n_blocks,),
        in_specs=[
            pl.BlockSpec((MOE_BLOCK, D_MODEL // 2), lambda i, be, nu, *_: (jnp.minimum(i, nu[0] - 1), 0)),
            pl.BlockSpec(memory_space=pl.ANY),
            pl.BlockSpec((1, 1, 1, 2 * d_exp), of_expert),
            pl.BlockSpec(memory_space=pl.ANY),
            pl.BlockSpec((1, 1, 1, D_MODEL), of_expert),
        ],
        out_specs=pl.BlockSpec((MOE_BLOCK, D_MODEL // 2), lambda i, *_: (i, 0)),
        scratch_shapes=[pltpu.VMEM((2, D_MODEL, 2 * d_exp), F32), pltpu.VMEM((2, d_exp, D_MODEL), F32),
                        pltpu.SemaphoreType.DMA((2, 2))],
    )
    return pl.pallas_call(
        functools.partial(_expert_kernel, layer),
        grid_spec=grid_spec,
        out_shape=jax.ShapeDtypeStruct((n_rows, D_MODEL // 2), jnp.uint32),
        compiler_params=pltpu.CompilerParams(dimension_semantics=("arbitrary",),
                                             vmem_limit_bytes=VMEM_LIMIT),
        name="moe_experts",
    )(block_e, n_used, valid, fresh, slot, next_e, xs, w_gu, b_gu.reshape(depth, n_exp, 1, -1),
      w_down, b_down.reshape(depth, n_exp, 1, -1))


def _combine_kernel(h_ref, gate_ref, y4_ref, lng_ref, lnb_ref, *rest):
    out_ref = rest[-1]
    gates = gate_ref[...]
    y_lo, y_hi = None, None
    for k in range(TOP_K):
        lo, hi = _unpack_halves(y4_ref[k])
        g = gates[:, k:k + 1]
        y_lo = g * lo if y_lo is None else y_lo + g * lo
        y_hi = g * hi if y_hi is None else y_hi + g * hi
    y = jnp.concatenate([y_lo, y_hi], axis=1)
    out_ref[...] = _layer_norm(ALPHA * h_ref[...] + y, lng_ref[...], lnb_ref[...])


def _combine(xt, gates, y4, part, earlier, ln_g, ln_b):
    t = xt.shape[0]
    tp = y4.shape[1]
    rows = min(COMBINE_ROWS, tp)
    first = part * (tp // rows)
    tok_spec = pl.BlockSpec((rows, D_MODEL), lambda i: (first + i, 0))
    in_specs = [tok_spec,
                pl.BlockSpec((rows, LANES), lambda i: (first + i, 0)),
                pl.BlockSpec((TOP_K, rows, D_MODEL // 2), lambda i: (0, i, 0)),
                _const_spec((1, D_MODEL)), _const_spec((1, D_MODEL))]
    args = [xt, gates, y4, ln_g.reshape(1, -1), ln_b.reshape(1, -1)]
    aliases = {}
    if earlier is not None:
        in_specs.append(pl.BlockSpec(memory_space=pl.ANY))
        args.append(earlier)
        aliases = {len(args) - 1: 0}
    return pl.pallas_call(
        _combine_kernel,
        grid=(tp // rows,),
        in_specs=in_specs,
        out_specs=tok_spec,
        out_shape=jax.ShapeDtypeStruct((t, D_MODEL), F32),
        input_output_aliases=aliases,
        compiler_params=pltpu.CompilerParams(dimension_semantics=("arbitrary",),
                                             vmem_limit_bytes=VMEM_LIMIT),
        name="moe_combine",
    )(*args)


def _moe_layer(h, layer, w_r, b_r, w_gu, b_gu, w_down, b_down, ln_g, ln_b):
    bsz, seq, d = h.shape
    t = bsz * seq
    xt = h.reshape(t, d)
    idx, gates, rank, counts, xp = _router(xt, w_r, b_r)
    experts = jnp.arange(N_EXPERTS, dtype=jnp.int32)
    padded = ((counts + MOE_BLOCK - 1) // MOE_BLOCK) * MOE_BLOCK
    pad_ends = jnp.sum(jnp.where(experts[None, :] <= experts[:, None], padded[None, :], 0), axis=1)
    pad_starts = pad_ends - padded
    dest = _dest_rows(pad_starts.astype(jnp.int32), idx, rank)[:TOP_K]
    dest_km = dest.reshape(t * TOP_K // SC_ROWS, SC_ROWS)
    n_blocks = -(-(t * TOP_K) // MOE_BLOCK) + N_EXPERTS
    block_lo = jnp.arange(n_blocks, dtype=jnp.int32) * MOE_BLOCK
    block_e = jnp.minimum(jnp.sum(pad_ends[None, :] <= block_lo[:, None], axis=1),
                          N_EXPERTS - 1).astype(jnp.int32)
    n_used = jnp.sum(padded, keepdims=True).astype(jnp.int32) // MOE_BLOCK
    of_block = block_e[:, None] == experts[None, :]
    rows_left = jnp.sum(jnp.where(of_block, (counts + pad_starts)[None, :], 0), axis=1) - block_lo
    valid = jnp.clip(rows_left, 0, MOE_BLOCK).astype(jnp.int32)
    xs = _dispatch(xp, dest_km, n_blocks * MOE_BLOCK)
    yb = _experts(xs, block_e, n_used, valid, layer, w_gu, b_gu, w_down, b_down)
    tp = t // COMBINE_PARTS
    out = None
    for p in range(COMBINE_PARTS):
        dest_p = dest[:, p * tp:(p + 1) * tp].reshape(tp * TOP_K // SC_ROWS, SC_ROWS)
        y4 = _gather_rows(yb, dest_p).reshape(TOP_K, tp, d // 2)
        out = _combine(xt, gates, y4, p, out, ln_g, ln_b)
    return out.reshape(bsz, seq, d)


def kernel(x, mix_w_in, sc_conv_w, ssm_conv_w, ssm_conv_b, ssm_dt_bias, ssm_a_log, ssm_d, ssm_norm_w, mix_w_out, conf_w_pw1, conf_b_pw1, conf_w_dw, conf_b_dw, conf_ln_g, conf_ln_b, conf_w_pw2, conf_b_pw2, router_w, router_b, exp_w_gu, exp_b_gu, exp_w_down, exp_b_down, ln_mix_g, ln_mix_b, ln_ffn_g, ln_ffn_b):
    h = x
    for i in range(DEPTH):
        j = i // 2
        if i % 2 == 0:
            h = _even_mixer(h, mix_w_in[j], sc_conv_w[j], ssm_conv_w[j], ssm_conv_b[j],
                            ssm_dt_bias[j], ssm_a_log[j], ssm_d[j], ssm_norm_w[j], mix_w_out[j],
                            ln_mix_g[i], ln_mix_b[i])
        else:
            h = _odd_mixer(h, conf_w_pw1[j], conf_b_pw1[j], conf_w_dw[j], conf_b_dw[j],
                           conf_ln_g[j], conf_ln_b[j], conf_w_pw2[j], conf_b_pw2[j],
                           ln_mix_g[i], ln_mix_b[i])
        h = _moe_layer(h, i, router_w[i], router_b[i], exp_w_gu, exp_b_gu, exp_w_down,
                       exp_b_down, ln_ffn_g[i], ln_ffn_b[i])
    return h
```

```python
import functools
import math

import jax
import jax.numpy as jnp
from jax import lax
from jax.experimental import pallas as pl
from jax.experimental.pallas import tpu as pltpu
from jax.experimental.pallas import tpu_sc as plsc

F32 = jnp.float32
BF16 = jnp.bfloat16

D_MODEL = 1024
DEPTH = 4
ALPHA = (2 * DEPTH) ** 0.25
LN_EPS = 1e-5
SC_KERNEL = 3
SSM_HEAD_DIM = 64
SSM_HEADS = 16
SSM_GROUPS = 2
SSM_STATE = 128
SSM_CONV = 4
SSM_CHUNK = 128
SSM_INNER = D_MODEL
GROUP_COLS = SSM_INNER // SSM_GROUPS
XBC_COLS = SSM_INNER + 2 * SSM_GROUPS * SSM_STATE
CONF_KERNEL = 31
N_EXPERTS = 32
TOP_K = 4
SWIGLU_LIMIT = 7.0
SWIGLU_ALPHA = 1.702
MOE_BLOCK = 512

LANES = 128
SUBLANES = 8
CARRY_ROWS = 8
CONF_CARRY_ROWS = 32
CONF_TILE_ROWS = 32
VMEM_LIMIT = 56 * 1024 * 1024

MIX_ROWS = 512
SSD_MIX_ROWS = 512
ROUTER_ROWS = 512
COMBINE_ROWS = 512
COMBINE_PARTS = 4
SC_ROWS = 64


def _layer_norm(v, g, b):
    mu = jnp.mean(v, axis=-1, keepdims=True)
    d = v - mu
    var = jnp.mean(d * d, axis=-1, keepdims=True)
    return d * lax.rsqrt(var + LN_EPS) * g + b


def _silu(v):
    return v * jax.nn.sigmoid(v)


def _softplus(v):
    return jnp.maximum(v, 0.0) + jnp.log1p(jnp.exp(-jnp.abs(v)))


def _dot(a, b):
    return jnp.dot(a, b, preferred_element_type=F32)


def _dot_exact01(m01, v):
    v1 = v.astype(BF16)
    r1 = v - v1.astype(F32)
    v2 = r1.astype(BF16)
    v3 = (r1 - v2.astype(F32)).astype(BF16)
    return _dot(m01, v1) + _dot(m01, v2) + _dot(m01, v3)


def _even_mixer_kernel(x_ref, w_sc_ref, w_z_ref, w_xbc_ref, w_dt_ref, scw_ref, ssmw_ref,
                       ssmb_ref, dtb_ref, a_ref, dexp_ref, normw_ref, wout_ref,
                       lng_ref, lnb_ref, out_ref,
                       u_buf, xbc_buf, act_buf, dtn_buf, y_buf, state_ref):
    rows = x_ref.shape[1]
    c = pl.program_id(1)

    @pl.when(c == 0)
    def _():
        u_buf[0:CARRY_ROWS, :] = jnp.zeros((CARRY_ROWS, D_MODEL), F32)
        xbc_buf[0:CARRY_ROWS, :] = jnp.zeros((CARRY_ROWS, XBC_COLS), F32)
        state_ref[...] = jnp.zeros(state_ref.shape, F32)

    x = x_ref[0]
    xb = x.astype(BF16)

    p = _dot(xb, w_sc_ref[...])
    u = p[:, D_MODEL:2 * D_MODEL] * p[:, :D_MODEL]
    u_buf[CARRY_ROWS:CARRY_ROWS + rows, :] = u
    conv = scw_ref[2:3, :] * u
    for k in range(SC_KERNEL - 1):
        shift = SC_KERNEL - 1 - k
        conv = conv + scw_ref[k:k + 1, :] * u_buf[CARRY_ROWS - shift:CARRY_ROWS - shift + rows, :]
    y_sc = p[:, 2 * D_MODEL:] * conv
    u_buf[0:CARRY_ROWS, :] = u[rows - CARRY_ROWS:, :]

    xbc = _dot(xb, w_xbc_ref[...])
    xbc_buf[CARRY_ROWS:CARRY_ROWS + rows, :] = xbc
    cv = ssmw_ref[SSM_CONV - 1:SSM_CONV, :] * xbc + ssmb_ref[...]
    for k in range(SSM_CONV - 1):
        shift = SSM_CONV - 1 - k
        cv = cv + ssmw_ref[k:k + 1, :] * xbc_buf[CARRY_ROWS - shift:CARRY_ROWS - shift + rows, :]
    act_buf[...] = _silu(cv)
    xbc_buf[0:CARRY_ROWS, :] = xbc[rows - CARRY_ROWS:, :]

    dtn_buf[...] = _softplus(_dot(xb, w_dt_ref[...]) + dtb_ref[...])

    li = lax.broadcasted_iota(jnp.int32, (SSM_CHUNK, SSM_CHUNK), 0)
    si = lax.broadcasted_iota(jnp.int32, (SSM_CHUNK, SSM_CHUNK), 1)
    causal = li >= si
    tril = jnp.where(causal, 1.0, 0.0).astype(BF16)
    low_half = si < SSM_HEAD_DIM
    low_row = low_half[0:1, :]

    for j in range(rows // SSM_CHUNK):
        r0 = j * SSM_CHUNK
        dt_c = dtn_buf[r0:r0 + SSM_CHUNK, :]
        acum = _dot_exact01(tril, dt_c * a_ref[...])
        alast = acum[SSM_CHUNK - 1:SSM_CHUNK, :]
        acum_t = acum.T
        dt_t = dt_c.T
        e_acum = jnp.exp(acum)
        w_t = (jnp.exp(alast - acum) * dt_c).T
        chunk_decay = jnp.exp(alast)
        xs_b = act_buf[r0:r0 + SSM_CHUNK, 0:SSM_INNER].astype(BF16)
        for g in range(SSM_GROUPS):
            b_g = act_buf[r0:r0 + SSM_CHUNK, SSM_INNER + g * SSM_STATE:SSM_INNER + (g + 1) * SSM_STATE]
            c_lo = SSM_INNER + (SSM_GROUPS + g) * SSM_STATE
            c_g = act_buf[r0:r0 + SSM_CHUNK, c_lo:c_lo + SSM_STATE]
            cb = lax.dot_general(c_g.astype(BF16), b_g.astype(BF16), (((1,), (1,)), ((), ())),
                                 preferred_element_type=F32)
            b_gt = b_g.T
            for pair in range(GROUP_COLS // LANES):
                col0 = g * GROUP_COLS + pair * LANES
                h0 = col0 // SSM_HEAD_DIM
                xs_p = xs_b[:, col0:col0 + LANES]
                zero_b = jnp.zeros_like(xs_p)
                xs_lo = jnp.where(low_half, xs_p, zero_b)
                xs_hi = jnp.where(low_half, zero_b, xs_p)
                st = state_ref[g, :, pair * LANES:(pair + 1) * LANES]
                st_b = st.astype(BF16)
                lhs, lhs_s = [], []
                for h in (h0, h0 + 1):
                    seg = acum[:, h:h + 1] - acum_t[h:h + 1, :]
                    decay = jnp.exp(jnp.where(causal, seg, -jnp.inf))
                    lhs.append((cb * decay * dt_t[h:h + 1, :]).astype(BF16))
                    lhs_s.append((b_gt * w_t[h:h + 1, :]).astype(BF16))
                for h in (h0, h0 + 1):
                    lhs.append((c_g * e_acum[:, h:h + 1]).astype(BF16))
                rhs = jnp.concatenate([xs_lo, xs_hi, jnp.where(low_half, st_b, zero_b),
                                       jnp.where(low_half, zero_b, st_b)], axis=0)
                y_buf[r0:r0 + SSM_CHUNK, col0:col0 + LANES] = _dot(
                    jnp.concatenate(lhs, axis=1), rhs)
                cd_p = jnp.where(low_row, chunk_decay[:, h0:h0 + 1], chunk_decay[:, h0 + 1:h0 + 2])
                state_ref[g, :, pair * LANES:(pair + 1) * LANES] = st * cd_p + _dot(
                    jnp.concatenate(lhs_s, axis=1), jnp.concatenate([xs_lo, xs_hi], axis=0))

    z = _dot(xb, w_z_ref[...])
    y = (y_buf[...] + act_buf[:, 0:SSM_INNER] * dexp_ref[...]) * _silu(z)
    parts = []
    for g in range(SSM_GROUPS):
        yg = y[:, g * GROUP_COLS:(g + 1) * GROUP_COLS]
        ms = jnp.mean(yg * yg, axis=-1, keepdims=True)
        parts.append(yg * lax.rsqrt(ms + LN_EPS) * normw_ref[:, g * GROUP_COLS:(g + 1) * GROUP_COLS])
    mix = _dot(y_sc.astype(BF16), wout_ref[0:D_MODEL, :])
    for g in range(SSM_GROUPS):
        lo = D_MODEL + g * GROUP_COLS
        mix = mix + _dot(parts[g].astype(BF16), wout_ref[lo:lo + GROUP_COLS, :])
    out_ref[0] = _layer_norm(ALPHA * x + mix, lng_ref[...], lnb_ref[...])


def _const_spec(shape):
    nd = len(shape)
    return pl.BlockSpec(shape, lambda *_: (0,) * nd, pipeline_mode=pl.Buffered(1))


def _even_mixer(h, w_in, sc_conv_w, ssm_conv_w, ssm_conv_b, dt_bias, a_log, d_skip, norm_w,
                w_out, ln_g, ln_b):
    bsz, seq, _ = h.shape
    rows = min(SSD_MIX_ROWS, seq)
    cut_z = 3 * D_MODEL
    cut_xbc = cut_z + SSM_INNER
    cut_dt = cut_xbc + XBC_COLS
    w_sc = w_in[:, :cut_z].astype(BF16)
    w_z = w_in[:, cut_z:cut_xbc].astype(BF16)
    w_xbc = w_in[:, cut_xbc:cut_dt].astype(BF16)
    pad = LANES - SSM_HEADS
    w_dt = jnp.pad(w_in[:, cut_dt:], ((0, 0), (0, pad))).astype(BF16)
    dtb = jnp.pad(dt_bias, (0, pad)).reshape(1, LANES)
    a = -jnp.exp(a_log)
    a_n = jnp.pad(a, (0, pad)).reshape(1, LANES)
    d_exp = jnp.repeat(d_skip, SSM_HEAD_DIM).reshape(1, SSM_INNER)
    consts = [w_sc, w_z, w_xbc, w_dt, sc_conv_w, ssm_conv_w, ssm_conv_b.reshape(1, -1), dtb, a_n,
              d_exp, norm_w.reshape(1, -1), w_out.astype(BF16), ln_g.reshape(1, -1),
              ln_b.reshape(1, -1)]
    x_spec = pl.BlockSpec((1, rows, D_MODEL), lambda b, c: (b, c, 0))
    return pl.pallas_call(
        _even_mixer_kernel,
        grid=(bsz, seq // rows),
        in_specs=[x_spec] + [_const_spec(v.shape) for v in consts],
        out_specs=x_spec,
        out_shape=jax.ShapeDtypeStruct(h.shape, F32),
        scratch_shapes=[
            pltpu.VMEM((CARRY_ROWS + rows, D_MODEL), F32),
            pltpu.VMEM((CARRY_ROWS + rows, XBC_COLS), F32),
            pltpu.VMEM((rows, XBC_COLS), F32),
            pltpu.VMEM((rows, LANES), F32),
            pltpu.VMEM((rows, SSM_INNER), F32),
            pltpu.VMEM((SSM_GROUPS, SSM_STATE, GROUP_COLS), F32),
        ],
        compiler_params=pltpu.CompilerParams(
            dimension_semantics=("arbitrary", "arbitrary"), vmem_limit_bytes=VMEM_LIMIT),
        name="even_mixer",
    )(h, *consts)


def _odd_mixer_kernel(x_ref, w1_ref, b1_ref, wdw_ref, bdw_ref, g_ref, b_ref, w2_ref, b2_ref,
                      lng_ref, lnb_ref, out_ref, sh_buf, conv_buf):
    rows = x_ref.shape[1]
    c = pl.program_id(1)

    @pl.when(c == 0)
    def _():
        sh_buf[0, 0:CONF_CARRY_ROWS, :] = jnp.zeros((CONF_CARRY_ROWS, D_MODEL), F32)

    x = x_ref[0]
    u = _dot(x.astype(BF16), w1_ref[...]) + b1_ref[...]
    u = u[:, :D_MODEL] * jax.nn.sigmoid(u[:, D_MODEL:])
    sh_buf[0, CONF_CARRY_ROWS:CONF_CARRY_ROWS + rows, :] = u
    span = rows + CONF_CARRY_ROWS - SUBLANES
    for r in range(1, SUBLANES):
        sh_buf[r, 0:span, :] = sh_buf[0, r:r + span, :]

    def conv_tile(i, carry):
        base = pl.multiple_of(i * CONF_TILE_ROWS, CONF_TILE_ROWS)
        groups = CONF_TILE_ROWS // SUBLANES
        acc = [bdw_ref[...] for _ in range(groups)]
        for k in range(CONF_KERNEL):
            lo = CONF_CARRY_ROWS - (CONF_KERNEL - 1 - k)
            w_k = wdw_ref[k]
            for g in range(groups):
                row0 = base + (lo // SUBLANES + g) * SUBLANES
                acc[g] = acc[g] + w_k * sh_buf[lo % SUBLANES, pl.ds(row0, SUBLANES), :]
        for g in range(groups):
            conv_buf[pl.ds(base + g * SUBLANES, SUBLANES), :] = acc[g]
        return carry

    lax.fori_loop(0, rows // CONF_TILE_ROWS, conv_tile, 0)
    sh_buf[0, 0:CONF_CARRY_ROWS, :] = u[rows - CONF_CARRY_ROWS:, :]
    v = _silu(_layer_norm(conv_buf[...], g_ref[...], b_ref[...]))
    mix = _dot(v.astype(BF16), w2_ref[...]) + b2_ref[...]
    out_ref[0] = _layer_norm(ALPHA * x + mix, lng_ref[...], lnb_ref[...])


def _odd_mixer(h, w_pw1, b_pw1, w_dw, b_dw, ln_g_c, ln_b_c, w_pw2, b_pw2, ln_g, ln_b):
    bsz, seq, _ = h.shape
    rows = min(MIX_ROWS, seq)
    w_dw_rows = jnp.broadcast_to(w_dw[:, None, :], (CONF_KERNEL, SUBLANES, D_MODEL))
    b_dw_rows = jnp.broadcast_to(b_dw[None, :], (SUBLANES, D_MODEL))
    consts = [w_pw1.astype(BF16), b_pw1.reshape(1, -1), w_dw_rows, b_dw_rows,
              ln_g_c.reshape(1, -1), ln_b_c.reshape(1, -1), w_pw2.astype(BF16),
              b_pw2.reshape(1, -1), ln_g.reshape(1, -1), ln_b.reshape(1, -1)]
    x_spec = pl.BlockSpec((1, rows, D_MODEL), lambda b, c: (b, c, 0))
    return pl.pallas_call(
        _odd_mixer_kernel,
        grid=(bsz, seq // rows),
        in_specs=[x_spec] + [_const_spec(v.shape) for v in consts],
        out_specs=x_spec,
        out_shape=jax.ShapeDtypeStruct(h.shape, F32),
        scratch_shapes=[pltpu.VMEM((SUBLANES, CONF_CARRY_ROWS + rows, D_MODEL), F32),
                        pltpu.VMEM((rows, D_MODEL), F32)],
        compiler_params=pltpu.CompilerParams(
            dimension_semantics=("arbitrary", "arbitrary"), vmem_limit_bytes=VMEM_LIMIT),
        name="odd_mixer",
    )(h, *consts)


def _pack_halves(v):
    n = v.shape[1] // 2
    lo = lax.bitcast_convert_type(v[:, :n].astype(BF16).astype(F32), jnp.uint32)
    hi = lax.bitcast_convert_type(v[:, n:].astype(BF16).astype(F32), jnp.uint32)
    return (lo >> 16) | (hi & jnp.uint32(0xFFFF0000))


def _unpack_halves(w):
    lo = lax.bitcast_convert_type(w << 16, F32)
    hi = lax.bitcast_convert_type(w & jnp.uint32(0xFFFF0000), F32)
    return lo, hi


def _router_kernel(x_ref, w_hi_ref, w_lo_ref, b_ref, idx_ref, gate_ref, rank_ref, cnt_ref, xp_ref,
                   carry_ref):
    rows = x_ref.shape[0]
    i = pl.program_id(0)

    @pl.when(i == 0)
    def _():
        carry_ref[...] = jnp.zeros(carry_ref.shape, F32)

    x = x_ref[...]
    x1 = x.astype(BF16)
    x2 = (x - x1.astype(F32)).astype(BF16)
    logits = (_dot(x1, w_hi_ref[...]) + (_dot(x1, w_lo_ref[...]) + _dot(x2, w_hi_ref[...]))
              + b_ref[...])
    xp_ref[...] = _pack_halves(x)
    lane = lax.broadcasted_iota(jnp.int32, (rows, LANES), 1)
    lane_f = lane.astype(F32)
    work = logits
    vals, sels, idxs = [], [], []
    for _ in range(TOP_K):
        m = jnp.max(work, axis=-1, keepdims=True)
        first = jnp.min(jnp.where(work == m, lane_f, float(LANES)), axis=-1, keepdims=True)
        sel = lane_f == first
        work = jnp.where(sel, -jnp.inf, work)
        vals.append(m)
        sels.append(sel)
        idxs.append(first)
    exps = [jnp.exp(v - vals[0]) for v in vals]
    inv = 1.0 / (exps[0] + exps[1] + exps[2] + exps[3])
    onehot = jnp.zeros((rows, LANES), F32)
    for sel in sels:
        onehot = onehot + jnp.where(sel, 1.0, 0.0)
    ri = lax.broadcasted_iota(jnp.int32, (rows, rows), 0)
    ci = lax.broadcasted_iota(jnp.int32, (rows, rows), 1)
    strict = jnp.where(ri > ci, 1.0, 0.0).astype(BF16)
    before = _dot(strict, onehot.astype(BF16)) + carry_ref[0:1, :]
    idx_o = jnp.zeros((rows, LANES), F32)
    gate_o = jnp.zeros((rows, LANES), F32)
    rank_o = jnp.zeros((rows, LANES), F32)
    for k in range(TOP_K):
        rank_k = jnp.sum(jnp.where(sels[k], before, 0.0), axis=-1, keepdims=True)
        here = lane == k
        idx_o = jnp.where(here, idxs[k], idx_o)
        gate_o = jnp.where(here, exps[k] * inv, gate_o)
        rank_o = jnp.where(here, rank_k, rank_o)
    gate_ref[...] = gate_o
    idx_ref[...] = idx_o.T[0:SUBLANES, :].astype(jnp.int32)
    rank_ref[...] = rank_o.T[0:SUBLANES, :].astype(jnp.int32)
    total = carry_ref[0:1, :] + jnp.sum(onehot, axis=0, keepdims=True)
    carry_ref[...] = jnp.broadcast_to(total, carry_ref.shape)
    cnt_ref[...] = jnp.broadcast_to(total, cnt_ref.shape).astype(jnp.int32)


def _router(xt, w_r, b_r):
    t = xt.shape[0]
    rows = min(ROUTER_ROWS, t)
    pad = LANES - N_EXPERTS
    w = jnp.pad(w_r, ((0, 0), (0, pad)))
    w_hi = w.astype(BF16)
    w_lo = (w - w_hi.astype(F32)).astype(BF16)
    b = jnp.pad(b_r, (0, pad), constant_values=-1e30).reshape(1, LANES)
    km_spec = pl.BlockSpec((SUBLANES, rows), lambda i: (0, i))
    idx, gate, rank, cnt, xp = pl.pallas_call(
        _router_kernel,
        grid=(t // rows,),
        in_specs=[pl.BlockSpec((rows, D_MODEL), lambda i: (i, 0)), _const_spec(w.shape),
                  _const_spec(w.shape), _const_spec(b.shape)],
        out_specs=[km_spec, pl.BlockSpec((rows, LANES), lambda i: (i, 0)), km_spec,
                   pl.BlockSpec((SUBLANES, LANES), lambda i: (0, 0)),
                   pl.BlockSpec((rows, D_MODEL // 2), lambda i: (i, 0))],
        out_shape=[jax.ShapeDtypeStruct((SUBLANES, t), jnp.int32),
                   jax.ShapeDtypeStruct((t, LANES), F32),
                   jax.ShapeDtypeStruct((SUBLANES, t), jnp.int32),
                   jax.ShapeDtypeStruct((SUBLANES, LANES), jnp.int32),
                   jax.ShapeDtypeStruct((t, D_MODEL // 2), jnp.uint32)],
        scratch_shapes=[pltpu.VMEM((SUBLANES, LANES), F32)],
        compiler_params=pltpu.CompilerParams(dimension_semantics=("arbitrary",),
                                             vmem_limit_bytes=VMEM_LIMIT),
        name="moe_router",
    )(xt, w_hi, w_lo, b)
    return idx, gate, rank, cnt[0, :N_EXPERTS], xp


def _dest_kernel(starts_ref, idx_ref, rank_ref, dest_ref):
    acc = rank_ref[...]
    idx = idx_ref[...]
    for e in range(N_EXPERTS):
        acc = acc + jnp.where(idx == e, starts_ref[e], 0)
    dest_ref[...] = acc


def _dest_rows(pad_starts, idx, rank):
    return pl.pallas_call(
        _dest_kernel,
        in_specs=[pl.BlockSpec(memory_space=pltpu.SMEM), pl.BlockSpec(idx.shape, lambda: (0, 0)),
                  pl.BlockSpec(idx.shape, lambda: (0, 0))],
        out_specs=pl.BlockSpec(idx.shape, lambda: (0, 0)),
        out_shape=jax.ShapeDtypeStruct(idx.shape, jnp.int32),
        name="moe_dest",
    )(pad_starts, idx, rank)


def _sc_workers():
    mesh = plsc.VectorSubcoreMesh(core_axis_name="c", subcore_axis_name="s")
    return mesh, mesh.num_cores, mesh.num_cores * mesh.num_subcores


def _dispatch(xt, dest_km, n_rows):
    t = xt.shape[0]
    mesh, n_cores, n_workers = _sc_workers()
    n_ch = t // n_workers // SC_ROWS
    assert n_ch * SC_ROWS * n_workers == t and n_ch % 2 == 0

    def body(x_hbm, dest_hbm, xs_hbm, idx_v, rows_v, load_sem, scat_sem):
        wid = lax.axis_index("s") * n_cores + lax.axis_index("c")
        base = wid * (n_ch * SC_ROWS)
        for k in range(TOP_K):
            pltpu.sync_copy(dest_hbm.at[pl.ds(k * (t // SC_ROWS) + wid * n_ch, n_ch)], idx_v.at[k])

        def load(j, b):
            off = pl.multiple_of(base + j * SC_ROWS, SC_ROWS)
            return pltpu.make_async_copy(x_hbm.at[pl.ds(off, SC_ROWS)], rows_v.at[b], load_sem.at[b])

        def scatter(j, b, k):
            return pltpu.make_async_copy(rows_v.at[b], xs_hbm.at[idx_v.at[k, j]], scat_sem.at[b])

        load(0, 0).start()

        @pl.loop(0, n_ch, step=2)
        def _(j0):
            for b in range(2):
                j = j0 + b
                load(j, b).wait()

                @pl.when(j >= 1)
                def _():
                    for k in range(TOP_K):
                        scatter(j - 1, 1 - b, k).wait()

                @pl.when(j + 1 < n_ch)
                def _():
                    load(j + 1, 1 - b).start()

                for k in range(TOP_K):
                    scatter(j, b, k).start()

        for k in range(TOP_K):
            scatter(n_ch - 1, 1, k).wait()

    return pl.kernel(
        body, mesh=mesh,
        out_type=jax.ShapeDtypeStruct((n_rows, xt.shape[1]), xt.dtype),
        scratch_types=[pltpu.VMEM((TOP_K, n_ch, SC_ROWS), jnp.int32),
                       pltpu.VMEM((2, SC_ROWS, xt.shape[1]), xt.dtype),
                       pltpu.SemaphoreType.DMA((2,)), pltpu.SemaphoreType.DMA((2,))],
    )(xt, dest_km)


def _gather_rows(yb, dest_km):
    n_idx = dest_km.shape[0] * SC_ROWS
    mesh, n_cores, n_workers = _sc_workers()
    n_ch = n_idx // n_workers // SC_ROWS
    assert n_ch * SC_ROWS * n_workers == n_idx and n_ch % 2 == 0

    def body(yb_hbm, dest_hbm, out_hbm, idx_v, rows_v, gat_sem, store_sem):
        wid = lax.axis_index("s") * n_cores + lax.axis_index("c")
        base = wid * (n_ch * SC_ROWS)
        pltpu.sync_copy(dest_hbm.at[pl.ds(wid * n_ch, n_ch)], idx_v)

        def gather(j, b):
            return pltpu.make_async_copy(yb_hbm.at[idx_v.at[j]], rows_v.at[b], gat_sem.at[b])

        def store(j, b):
            off = pl.multiple_of(base + j * SC_ROWS, SC_ROWS)
            return pltpu.make_async_copy(rows_v.at[b], out_hbm.at[pl.ds(off, SC_ROWS)], store_sem.at[b])

        gather(0, 0).start()

        @pl.loop(0, n_ch, step=2)
        def _(j0):
            for b in range(2):
                j = j0 + b
                gather(j, b).wait()

                @pl.when(j >= 1)
                def _():
                    store(j - 1, 1 - b).wait()

                @pl.when(j + 1 < n_ch)
                def _():
                    gather(j + 1, 1 - b).start()

                store(j, b).start()

        store(n_ch - 1, 1).wait()

    return pl.kernel(
        body, mesh=mesh,
        out_type=jax.ShapeDtypeStruct((n_idx, yb.shape[1]), yb.dtype),
        scratch_types=[pltpu.VMEM((n_ch, SC_ROWS), jnp.int32),
                       pltpu.VMEM((2, SC_ROWS, yb.shape[1]), yb.dtype),
                       pltpu.SemaphoreType.DMA((2,)), pltpu.SemaphoreType.DMA((2,))],
    )(yb, dest_km)


def _expert_kernel(layer, be_ref, nused_ref, valid_ref, fresh_ref, slot_ref, next_ref,
                   x_ref, wgu_hbm, bgu_ref, wd_hbm, bd_ref, out_ref, wgu_buf, wd_buf, sem):
    i = pl.program_id(0)
    used = i < nused_ref[0]

    def weight_copies(expert, slot):
        return (pltpu.make_async_copy(wgu_hbm.at[layer, expert], wgu_buf.at[slot], sem.at[0, slot]),
                pltpu.make_async_copy(wd_hbm.at[layer, expert], wd_buf.at[slot], sem.at[1, slot]))

    @pl.when(i == 0)
    def _():
        for cp in weight_copies(be_ref[0], 0):
            cp.start()

    @pl.when(jnp.logical_and(used, fresh_ref[i] == 1))
    def _():
        for cp in weight_copies(be_ref[i], slot_ref[i]):
            cp.wait()

        @pl.when(next_ref[i] >= 0)
        def _():
            for cp in weight_copies(next_ref[i], 1 - slot_ref[i]):
                cp.start()

    @pl.when(used)
    def _():
        slot = slot_ref[i]
        row = lax.broadcasted_iota(jnp.int32, x_ref.shape, 0)
        x_lo, x_hi = _unpack_halves(jnp.where(row < valid_ref[i], x_ref[...], jnp.uint32(0)))
        half = x_ref.shape[1]
        gu = (_dot(x_lo.astype(BF16), wgu_buf[slot, 0:half, :].astype(BF16))
              + _dot(x_hi.astype(BF16), wgu_buf[slot, half:, :].astype(BF16)) + bgu_ref[0, 0])
        d_exp = wd_buf.shape[1]
        gate = jnp.minimum(gu[:, :d_exp], SWIGLU_LIMIT)
        up = jnp.clip(gu[:, d_exp:], -SWIGLU_LIMIT, SWIGLU_LIMIT)
        act = (up + 1.0) * (gate * jax.nn.sigmoid(SWIGLU_ALPHA * gate))
        out_ref[...] = _pack_halves(_dot(act.astype(BF16), wd_buf[slot].astype(BF16)) + bd_ref[0, 0])

    @pl.when(jnp.logical_not(used))
    def _():
        out_ref[...] = jnp.zeros(out_ref.shape, jnp.uint32)


def _experts(xs, block_e, n_used, valid, layer, w_gu, b_gu, w_down, b_down):
    n_rows = xs.shape[0]
    n_blocks = n_rows // MOE_BLOCK
    depth, n_exp, d_exp = w_down.shape[:3]
    blocks = jnp.arange(n_blocks, dtype=jnp.int32)
    fresh = jnp.concatenate([jnp.ones((1,), jnp.int32),
                             (block_e[1:] != block_e[:-1]).astype(jnp.int32)])
    ordinal = jnp.sum(jnp.where(blocks[None, :] <= blocks[:, None], fresh[None, :], 0), axis=1) - 1
    slot = (ordinal % 2).astype(jnp.int32)
    follow = jnp.sum((block_e[None, :] <= block_e[:, None]).astype(jnp.int32), axis=1)
    follow_e = jnp.sum(jnp.where(blocks[None, :] == follow[:, None], block_e[None, :], 0), axis=1)
    next_e = jnp.where(follow < n_used[0], follow_e, -1).astype(jnp.int32)

    def of_expert(i, be, *_):
        return (layer, be[i], 0, 0)

    grid_spec = pltpu.PrefetchScalarGridSpec(
        num_scalar_prefetch=6,
        grid=(n_blocks,),
        in_specs=[
            pl.BlockSpec((MOE_BLOCK, D_MODEL // 2), lambda i, be, nu, *_: (jnp.minimum(i, nu[0] - 1), 0)),
            pl.BlockSpec(memory_space=pl.ANY),
            pl.BlockSpec((1, 1, 1, 2 * d_exp), of_expert),
            pl.BlockSpec(memory_space=pl.ANY),
            pl.BlockSpec((1, 1, 1, D_MODEL), of_expert),
        ],
        out_specs=pl.BlockSpec((MOE_BLOCK, D_MODEL // 2), lambda i, *_: (i, 0)),
        scratch_shapes=[pltpu.VMEM((2, D_MODEL, 2 * d_exp), F32), pltpu.VMEM((2, d_exp, D_MODEL), F32),
                        pltpu.SemaphoreType.DMA((2, 2))],
    )
    return pl.pallas_call(
        functools.partial(_expert_kernel, layer),
        grid_spec=grid_spec,
        out_shape=jax.ShapeDtypeStruct((n_rows, D_MODEL // 2), jnp.uint32),
        compiler_params=pltpu.CompilerParams(dimension_semantics=("arbitrary",),
                                             vmem_limit_bytes=VMEM_LIMIT),
        name="moe_experts",
    )(block_e, n_used, valid, fresh, slot, next_e, xs, w_gu, b_gu.reshape(depth, n_exp, 1, -1),
      w_down, b_down.reshape(depth, n_exp, 1, -1))


def _combine_kernel(h_ref, gate_ref, y4_ref, lng_ref, lnb_ref, *rest):
    out_ref = rest[-1]
    gates = gate_ref[...]
    y_lo, y_hi = None, None
    for k in range(TOP_K):
        lo, hi = _unpack_halves(y4_ref[k])
        g = gates[:, k:k + 1]
        y_lo = g * lo if y_lo is None else y_lo + g * lo
        y_hi = g * hi if y_hi is None else y_hi + g * hi
    y = jnp.concatenate([y_lo, y_hi], axis=1)
    out_ref[...] = _layer_norm(ALPHA * h_ref[...] + y, lng_ref[...], lnb_ref[...])


def _combine(xt, gates, y4, part, earlier, ln_g, ln_b):
    t = xt.shape[0]
    tp = y4.shape[1]
    rows = min(COMBINE_ROWS, tp)
    first = part * (tp // rows)
    tok_spec = pl.BlockSpec((rows, D_MODEL), lambda i: (first + i, 0))
    in_specs = [tok_spec,
                pl.BlockSpec((rows, LANES), lambda i: (first + i, 0)),
                pl.BlockSpec((TOP_K, rows, D_MODEL // 2), lambda i: (0, i, 0)),
                _const_spec((1, D_MODEL)), _const_spec((1, D_MODEL))]
    args = [xt, gates, y4, ln_g.reshape(1, -1), ln_b.reshape(1, -1)]
    aliases = {}
    if earlier is not None:
        in_specs.append(pl.BlockSpec(memory_space=pl.ANY))
        args.append(earlier)
        aliases = {len(args) - 1: 0}
    return pl.pallas_call(
        _combine_kernel,
        grid=(tp // rows,),
        in_specs=in_specs,
        out_specs=tok_spec,
        out_shape=jax.ShapeDtypeStruct((t, D_MODEL), F32),
        input_output_aliases=aliases,
        compiler_params=pltpu.CompilerParams(dimension_semantics=("arbitrary",),
                                             vmem_limit_bytes=VMEM_LIMIT),
        name="moe_combine",
    )(*args)


def _moe_layer(h, layer, w_r, b_r, w_gu, b_gu, w_down, b_down, ln_g, ln_b):
    bsz, seq, d = h.shape
    t = bsz * seq
    xt = h.reshape(t, d)
    idx, gates, rank, counts, xp = _router(xt, w_r, b_r)
    experts = jnp.arange(N_EXPERTS, dtype=jnp.int32)
    padded = ((counts + MOE_BLOCK - 1) // MOE_BLOCK) * MOE_BLOCK
    pad_ends = jnp.sum(jnp.where(experts[None, :] <= experts[:, None], padded[None, :], 0), axis=1)
    pad_starts = pad_ends - padded
    dest = _dest_rows(pad_starts.astype(jnp.int32), idx, rank)[:TOP_K]
    dest_km = dest.reshape(t * TOP_K // SC_ROWS, SC_ROWS)
    n_blocks = -(-(t * TOP_K) // MOE_BLOCK) + N_EXPERTS
    block_lo = jnp.arange(n_blocks, dtype=jnp.int32) * MOE_BLOCK
    block_e = jnp.minimum(jnp.sum(pad_ends[None, :] <= block_lo[:, None], axis=1),
                          N_EXPERTS - 1).astype(jnp.int32)
    n_used = jnp.sum(padded, keepdims=True).astype(jnp.int32) // MOE_BLOCK
    of_block = block_e[:, None] == experts[None, :]
    rows_left = jnp.sum(jnp.where(of_block, (counts + pad_starts)[None, :], 0), axis=1) - block_lo
    valid = jnp.clip(rows_left, 0, MOE_BLOCK).astype(jnp.int32)
    xs = _dispatch(xp, dest_km, n_blocks * MOE_BLOCK)
    yb = _experts(xs, block_e, n_used, valid, layer, w_gu, b_gu, w_down, b_down)
    tp = t // COMBINE_PARTS
    out = None
    for p in range(COMBINE_PARTS):
        dest_p = dest[:, p * tp:(p + 1) * tp].reshape(tp * TOP_K // SC_ROWS, SC_ROWS)
        y4 = _gather_rows(yb, dest_p).reshape(TOP_K, tp, d // 2)
        out = _combine(xt, gates, y4, p, out, ln_g, ln_b)
    return out.reshape(bsz, seq, d)


def kernel(x, mix_w_in, sc_conv_w, ssm_conv_w, ssm_conv_b, ssm_dt_bias, ssm_a_log, ssm_d, ssm_norm_w, mix_w_out, conf_w_pw1, conf_b_pw1, conf_w_dw, conf_b_dw, conf_ln_g, conf_ln_b, conf_w_pw2, conf_b_pw2, router_w, router_b, exp_w_gu, exp_b_gu, exp_w_down, exp_b_down, ln_mix_g, ln_mix_b, ln_ffn_g, ln_ffn_b):
    h = x
    for i in range(DEPTH):
        j = i // 2
        if i % 2 == 0:
            h = _even_mixer(h, mix_w_in[j], sc_conv_w[j], ssm_conv_w[j], ssm_conv_b[j],
                            ssm_dt_bias[j], ssm_a_log[j], ssm_d[j], ssm_norm_w[j], mix_w_out[j],
                            ln_mix_g[i], ln_mix_b[i])
        else:
            h = _odd_mixer(h, conf_w_pw1[j], conf_b_pw1[j], conf_w_dw[j], conf_b_dw[j],
                           conf_ln_g[j], conf_ln_b[j], conf_w_pw2[j], conf_b_pw2[j],
                           ln_mix_g[i], ln_mix_b[i])
        h = _moe_layer(h, i, router_w[i], router_b[i], exp_w_gu, exp_b_gu, exp_w_down,
                       exp_b_down, ln_ffn_g[i], ln_ffn_b[i])
    return h
```
